```python
import jax
import jax.numpy as jnp
from jax import lax
import numpy as np

D_MODEL = 4096
BATCH = 4
SEQ = 2048
DEPTH = 2
DEC_BATCH = 8
DEC_SEQ = 8
PAST_LEN = 16384
PAGE_SIZE = 128

F32 = jnp.float32
HEAD_DIM = 128
D_A = D_MODEL // 2
D_B = D_MODEL // 4
D_C = D_MODEL // 4
D_MIX = D_A + D_B + D_C
N_HEADS_A = D_A // HEAD_DIM
N_KV = 2
GQA = N_HEADS_A // N_KV
KV_W = N_KV * HEAD_DIM
N_BRANCH = 3
N_HEADS_C = D_C // HEAD_DIM
CMP_BLOCK = 32
CMP_STRIDE = 16
SLC_BLOCK = 64
N_SLC = 16
WINDOW = 512
SEL_QBLOCK = 64
WIN_QBLOCK = 128
FORCE_BONUS = 1.0e4
CONV_W = 31
CHUNK = 128
N_EXPERTS = 32
TOP_K = 4
D_FF = D_MODEL // 2
SWIGLU_ALPHA = 1.702
SWIGLU_LIMIT = 7.0
MOE_MIN_BLOCK = 8
MOE_MAX_BLOCK = 128
DEEPNORM_ALPHA = (2 * DEPTH) ** 0.25
DEEPNORM_BETA = (8 * DEPTH) ** -0.25
LN_EPS = 1e-5
NEG_INF = -1e30
SPLITS = (D_A, 6 * KV_W, N_BRANCH * N_HEADS_A, 2 * D_B, 2 * D_C)
D_IN = D_A + 6 * KV_W + N_BRANCH * N_HEADS_A + 2 * D_B + 2 * D_C

kernel_name = "hymba_nsa_conformer_gmlp_moe_step"


def layer_norm(x, g, b):
    xf = x.astype(F32)
    mu = jnp.mean(xf, axis=-1, keepdims=True)
    xc = xf - mu
    var = jnp.mean(xc * xc, axis=-1, keepdims=True)
    return xc * lax.rsqrt(var + LN_EPS) * g.astype(F32) + b.astype(F32)


def masked_softmax(s, mask):
    s = jnp.where(mask, s.astype(F32), NEG_INF)
    m = jnp.max(s, axis=-1, keepdims=True)
    p = jnp.exp(s - m) * mask
    return p / jnp.maximum(jnp.sum(p, axis=-1, keepdims=True), 1e-30)


def pad_rows(x, mult):
    extra = (-x.shape[1]) % mult
    return jnp.pad(x, [(0, 0), (0, extra)] + [(0, 0)] * (x.ndim - 2))


def split_proj(p):
    b_, t, _ = p.shape
    cuts = np.cumsum(SPLITS)[:-1].tolist()
    q, kv, g, pb, pc = jnp.split(p, cuts, axis=-1)
    q = q.reshape(b_, t, N_HEADS_A, HEAD_DIM)
    kv = kv.reshape(b_, t, 6, N_KV, HEAD_DIM)
    gates = jax.nn.sigmoid(g.astype(F32)).reshape(b_, t, N_BRANCH, N_HEADS_A)
    return q, kv, gates, pb, pc


def compress(k, pe, w):
    b_, length, kvh, hd = k.shape
    nch = length // CMP_STRIDE
    r = CMP_BLOCK // CMP_STRIDE
    nc = nch - r + 1
    ch = k.reshape(b_, nch, CMP_STRIDE, kvh, hd)
    blocks = jnp.concatenate([ch[:, s:s + nc] for s in range(r)], axis=2)
    blocks = blocks + pe[:, None, :].astype(k.dtype)
    flat = jnp.swapaxes(blocks, 2, 3).reshape(b_, nc, kvh, CMP_BLOCK * hd)
    return jnp.dot(flat, w.astype(k.dtype))


def nsa_cmp_slc(q, q_pos, kc, vc, ks, vs):
    b_, t, h, hd = q.shape
    nc = kc.shape[1]
    ns = ks.shape[1] // SLC_BLOCK
    n_sel = min(N_SLC, ns)
    scale = hd ** -0.5
    ksb = ks.reshape(b_, ns, SLC_BLOCK, N_KV, hd).transpose(0, 3, 1, 2, 4)
    vsb = vs.reshape(b_, ns, SLC_BLOCK, N_KV, hd).transpose(0, 3, 1, 2, 4)
    c_start = jnp.arange(nc, dtype=jnp.int32) * CMP_STRIDE
    c_end = c_start + CMP_BLOCK - 1
    s_start = jnp.arange(ns, dtype=jnp.int32) * SLC_BLOCK
    cover = ((c_start[:, None] < s_start[None, :] + SLC_BLOCK)
             & (c_start[:, None] + CMP_BLOCK > s_start[None, :])).astype(F32)
    bi = jnp.arange(b_)[:, None, None, None]
    gi = jnp.arange(N_KV)[None, :, None, None]
    blk_ids = jnp.arange(ns, dtype=jnp.int32)

    def one_block(args):
        qb, pos = args
        nq = pos.shape[0]
        qg = qb.reshape(b_, nq, N_KV, GQA, hd)
        s_c = jnp.einsum("bqkgd,bnkd->bkgqn", qg, kc) * scale
        p_c = masked_softmax(s_c, c_end[None, :] <= pos[:, None])
        o_c = jnp.einsum("bkgqn,bnkd->bqkgd", p_c.astype(vc.dtype), vc)
        imp = jnp.einsum("bkgqn,ns->bkqs", p_c, cover)
        cur = pos // SLC_BLOCK
        valid = blk_ids[None, :] <= cur[:, None]
        forced = (blk_ids[None, :] == 0) | (blk_ids[None, :] == cur[:, None]) | (blk_ids[None, :] == cur[:, None] - 1)
        score = jnp.where(valid, imp + jnp.where(forced, FORCE_BONUS, 0.0), NEG_INF)
        _, idx = lax.top_k(score, n_sel)
        kg = ksb[bi, gi, idx].reshape(b_, N_KV, nq, n_sel * SLC_BLOCK, hd)
        vg = vsb[bi, gi, idx].reshape(b_, N_KV, nq, n_sel * SLC_BLOCK, hd)
        kpos = idx[..., None] * SLC_BLOCK + jnp.arange(SLC_BLOCK, dtype=jnp.int32)
        m_s = (kpos <= pos[None, None, :, None, None]).reshape(b_, N_KV, 1, nq, n_sel * SLC_BLOCK)
        s_s = jnp.einsum("bqkgd,bkqmd->bkgqm", qg, kg) * scale
        p_s = masked_softmax(s_s, m_s)
        o_s = jnp.einsum("bkgqm,bkqmd->bqkgd", p_s.astype(vg.dtype), vg)
        return o_c.reshape(b_, nq, h, hd), o_s.reshape(b_, nq, h, hd)

    qblk = SEL_QBLOCK if t % SEL_QBLOCK == 0 else t
    nblk = t // qblk
    qs = jnp.swapaxes(q.reshape(b_, nblk, qblk, h, hd), 0, 1)
    o_c, o_s = lax.map(one_block, (qs, q_pos.reshape(nblk, qblk)))
    back = lambda o: jnp.swapaxes(o, 0, 1).reshape(b_, t, h, hd)
    return back(o_c), back(o_s)


def window_mask(tq, tk):
    return (tk <= tq) & (tk > tq - WINDOW) & (tk >= 0)


def window_attn_banded(q, k, v):
    b_, t, h, hd = q.shape
    nb = t // WIN_QBLOCK
    n_prev = -(-WINDOW // WIN_QBLOCK)

    def bands(x):
        xb = jnp.pad(x.reshape(b_, nb, WIN_QBLOCK, N_KV, hd), ((0, 0), (n_prev, 0), (0, 0), (0, 0), (0, 0)))
        return jnp.concatenate([xb[:, s:s + nb] for s in range(n_prev + 1)], axis=2)

    kw, vw = bands(k), bands(v)
    qpos = jnp.arange(t, dtype=jnp.int32).reshape(nb, WIN_QBLOCK)
    kpos = ((jnp.arange(nb, dtype=jnp.int32)[:, None] - n_prev) * WIN_QBLOCK
            + jnp.arange((n_prev + 1) * WIN_QBLOCK, dtype=jnp.int32)[None, :])
    mask = window_mask(qpos[:, :, None], kpos[:, None, :])
    qg = q.reshape(b_, nb, WIN_QBLOCK, N_KV, GQA, hd)
    s = jnp.einsum("bnqkgd,bnmkd->bnkgqm", qg, kw) * hd ** -0.5
    p = masked_softmax(s, mask[None, :, None, None])
    o = jnp.einsum("bnkgqm,bnmkd->bnqkgd", p.astype(vw.dtype), vw)
    return o.reshape(b_, t, h, hd)


def window_attn_direct(q, q_pos, k, v, k_pos):
    b_, t, h, hd = q.shape
    qg = q.reshape(b_, t, N_KV, GQA, hd)
    s = jnp.einsum("bqkgd,bmkd->bkgqm", qg, k) * hd ** -0.5
    p = masked_softmax(s, window_mask(q_pos[:, None], k_pos[None, :]))
    o = jnp.einsum("bkgqm,bmkd->bqkgd", p.astype(v.dtype), v)
    return o.reshape(b_, t, h, hd)


def nsa_merge(gates, o_cmp, o_slc, o_win):
    o = (gates[:, :, 0, :, None] * o_cmp + gates[:, :, 1, :, None] * o_slc
         + gates[:, :, 2, :, None] * o_win)
    return o.reshape(o.shape[0], o.shape[1], D_A)


def glu_in(pb):
    a, g = jnp.split(pb, 2, axis=-1)
    return a * jax.nn.sigmoid(g)


def conformer_conv(glu_ext, conv_w, conv_b, ln_g, ln_b):
    y = lax.conv_general_dilated(glu_ext, conv_w[:, None, :].astype(glu_ext.dtype), window_strides=(1,),
                                 padding="VALID", dimension_numbers=("NWC", "WIO", "NWC"),
                                 feature_group_count=D_B)
    return jax.nn.silu(layer_norm(y + conv_b, ln_g, ln_b))


def sgu_split(pc, ln_g, ln_b):
    u, v = jnp.split(jax.nn.gelu(pc), 2, axis=-1)
    return u, layer_norm(v, ln_g, ln_b).astype(pc.dtype)


def chunk_mix(v, w_s, b_s):
    b_, length, _ = v.shape
    vh = v.reshape(b_, length // CHUNK, CHUNK, N_HEADS_C, HEAD_DIM)
    wm = w_s * jnp.tril(jnp.ones((CHUNK, CHUNK), w_s.dtype))
    out = jnp.einsum("hts,bnshd->bnthd", wm.astype(v.dtype), vh) + jnp.swapaxes(b_s, 0, 1)[None, None, :, :, None]
    return out.reshape(b_, length, D_C)


def mixer_prompt(h, w_in, w_out, cmp_pe, cmp_w, conv_w, conv_b, conv_ln_g, conv_ln_b,
                 sgu_ln_g, sgu_ln_b, sgu_w, sgu_b):
    b_, t, _ = h.shape
    q, kv, gates, pb, pc = split_proj(jnp.dot(h, w_in))
    pos = jnp.arange(t, dtype=jnp.int32)
    kc = compress(kv[:, :, 0], cmp_pe[0], cmp_w[0])
    vc = compress(kv[:, :, 1], cmp_pe[1], cmp_w[1])
    o_cmp, o_slc = nsa_cmp_slc(q, pos, kc, vc, kv[:, :, 2], kv[:, :, 3])
    o_win = window_attn_banded(q, kv[:, :, 4], kv[:, :, 5])
    o_a = nsa_merge(gates, o_cmp, o_slc, o_win)
    glu = glu_in(pb)
    o_b = conformer_conv(jnp.pad(glu, ((0, 0), (CONV_W - 1, 0), (0, 0))), conv_w, conv_b, conv_ln_g, conv_ln_b)
    u, v = sgu_split(pc, sgu_ln_g, sgu_ln_b)
    o_c = u * chunk_mix(v, sgu_w, sgu_b)
    cat = jnp.concatenate([o_a.astype(h.dtype), o_b.astype(h.dtype), o_c.astype(h.dtype)], axis=-1)
    y = jnp.dot(cat, w_out)
    w_keep = min(WINDOW, t)
    new_state = (kv[:, :, 0:2], kv[:, :, 2:4], kv[:, t - w_keep:, 4:6], glu[:, t - (CONV_W - 1):])
    return y, new_state


def mixer_sample(h, past_cmp, past_slc, win_buf, conv_buf, w_in, w_out, cmp_pe, cmp_w, conv_w, conv_b,
                 conv_ln_g, conv_ln_b, sgu_ln_g, sgu_ln_b, sgu_w, sgu_b):
    b_, t, _ = h.shape
    n_past = past_cmp.shape[1]
    w_buf = win_buf.shape[1]
    q, kv, gates, pb, pc = split_proj(jnp.dot(h, w_in))
    pos = n_past + jnp.arange(t, dtype=jnp.int32)
    full_c = pad_rows(jnp.concatenate([past_cmp.astype(kv.dtype), kv[:, :, 0:2]], axis=1), CMP_STRIDE)
    kc = compress(full_c[:, :, 0], cmp_pe[0], cmp_w[0])
    vc = compress(full_c[:, :, 1], cmp_pe[1], cmp_w[1])
    full_s = pad_rows(jnp.concatenate([past_slc.astype(kv.dtype), kv[:, :, 2:4]], axis=1), SLC_BLOCK)
    o_cmp, o_slc = nsa_cmp_slc(q, pos, kc, vc, full_s[:, :, 0], full_s[:, :, 1])
    win = jnp.concatenate([win_buf.astype(kv.dtype), kv[:, :, 4:6]], axis=1)
    kpos = n_past - w_buf + jnp.arange(w_buf + t, dtype=jnp.int32)
    o_win = window_attn_direct(q, pos, win[:, :, 0], win[:, :, 1], kpos)
    o_a = nsa_merge(gates, o_cmp, o_slc, o_win)
    glu = glu_in(pb)
    glu_ext = jnp.concatenate([conv_buf.astype(glu.dtype), glu], axis=1)
    o_b = conformer_conv(glu_ext, conv_w, conv_b, conv_ln_g, conv_ln_b)
    u, v = sgu_split(pc, sgu_ln_g, sgu_ln_b)
    o_c = u * chunk_mix(pad_rows(v, CHUNK), sgu_w, sgu_b)[:, :t]
    cat = jnp.concatenate([o_a.astype(h.dtype), o_b.astype(h.dtype), o_c.astype(h.dtype)], axis=-1)
    y = jnp.dot(cat, w_out)
    new_state = (kv[:, :, 0:2], kv[:, :, 2:4], win[:, t:], glu_ext[:, t:], v)
    return y, new_state


def routed_moe(h, l, router_w, router_b, moe_w1, moe_b1, moe_w2, moe_b2):
    shp = h.shape
    xf = h.reshape(-1, shp[-1])
    n = xf.shape[0]
    n_assign = n * TOP_K
    logits = jnp.dot(xf, router_w).astype(F32) + router_b.astype(F32)
    top_val, top_exp = lax.top_k(logits, TOP_K)
    weights = jax.nn.softmax(top_val, axis=-1)
    flat_e = top_exp.reshape(-1)
    flat_t = jnp.repeat(jnp.arange(n, dtype=jnp.int32), TOP_K)
    flat_w = weights.reshape(-1)
    per_exp = max(1, n_assign // N_EXPERTS)
    blk = min(MOE_MAX_BLOCK, max(MOE_MIN_BLOCK, 1 << (per_exp.bit_length() - 1)))
    cap = -(-(n_assign + N_EXPERTS * (blk - 1)) // blk) * blk
    n_blk = cap // blk
    order = jnp.argsort(flat_e)
    e_sorted = flat_e[order]
    counts = jnp.bincount(flat_e, length=N_EXPERTS)
    padded = (counts + blk - 1) // blk * blk
    pad_end = jnp.cumsum(padded)
    pad_start = pad_end - padded
    raw_start = jnp.cumsum(counts) - counts
    dest = pad_start[e_sorted] + jnp.arange(n_assign) - raw_start[e_sorted]
    row_tok = jnp.zeros((cap,), jnp.int32).at[dest].set(flat_t[order])
    row_w = jnp.zeros((cap,), F32).at[dest].set(flat_w[order])
    blk_exp = jnp.minimum(jnp.searchsorted(pad_end, jnp.arange(n_blk) * blk, side="right"), N_EXPERTS - 1)

    def expert_block(args):
        e, toks, wts = args
        hb = jnp.dot(xf[toks], moe_w1[l, e]) + moe_b1[l, e]
        x_glu, x_lin = jnp.split(hb, 2, axis=-1)
        x_glu = jnp.minimum(x_glu, SWIGLU_LIMIT)
        x_lin = jnp.clip(x_lin, -SWIGLU_LIMIT, SWIGLU_LIMIT)
        act = x_glu * jax.nn.sigmoid(SWIGLU_ALPHA * x_glu) * (x_lin + 1.0)
        yb = jnp.dot(act, moe_w2[l, e]) + moe_b2[l, e]
        return yb.astype(F32) * wts[:, None]

    ys = lax.map(expert_block, (blk_exp, row_tok.reshape(n_blk, blk), row_w.reshape(n_blk, blk)))
    out = jnp.zeros((n, shp[-1]), F32).at[row_tok].add(ys.reshape(cap, shp[-1]))
    return out.reshape(shp)


def deepnorm(x, out, gate, g, b):
    return layer_norm(DEEPNORM_ALPHA * x + (1.0 + gate) * out, g, b).astype(x.dtype)


def decoder_layer(x, c, mixer, l, ada_w, ada_b, ln_g, ln_b, router_w, router_b,
                  moe_w1, moe_b1, moe_w2, moe_b2):
    mod = jnp.dot(jax.nn.silu(c), ada_w[l]) + ada_b[l]
    sh1, sc1, g1, sh2, sc2, g2 = jnp.split(mod[:, None, :], 6, axis=-1)
    m, new_state = mixer(x * (1.0 + sc1) + sh1)
    x = deepnorm(x, m, g1, ln_g[l, 0], ln_b[l, 0])
    f = routed_moe(x * (1.0 + sc2) + sh2, l, router_w[l], router_b[l], moe_w1, moe_b1, moe_w2, moe_b2)
    x = deepnorm(x, f, g2, ln_g[l, 1], ln_b[l, 1])
    return x, new_state


def setup_inputs(seed: int = 0) -> dict:
    key = jax.random.key(seed)
    keys = iter(jax.random.split(key, 40))

    def unif(shape, std):
        a = std * 3.0 ** 0.5
        return jax.random.uniform(next(keys), shape, F32, -a, a)

    def normal(shape, std=1.0):
        return jax.random.normal(next(keys), shape, F32) * std

    n_pages = PAST_LEN // PAGE_SIZE
    n_pool = (5 * DEC_BATCH * n_pages + 3) // 4
    w_buf = min(WINDOW, PAST_LEN)
    kv_row = (2, N_KV, HEAD_DIM)
    x_prompt = normal((BATCH, SEQ, D_MODEL))
    x_sample = normal((DEC_BATCH, DEC_SEQ, D_MODEL))
    cache_kv_cmp = normal((DEPTH, n_pool, PAGE_SIZE) + kv_row)
    cache_kv_slc = normal((DEPTH, n_pool, PAGE_SIZE) + kv_row)
    cache_kv_win = normal((DEPTH, DEC_BATCH, w_buf) + kv_row)
    state_conv = normal((DEPTH, DEC_BATCH, CONV_W - 1, D_B), 0.5)
    perm = jax.random.permutation(next(keys), n_pool)
    page_table = perm[: DEC_BATCH * n_pages].reshape(DEC_BATCH, n_pages).astype(jnp.int32)
    c_prompt = normal((BATCH, D_MODEL))
    c_sample = normal((DEC_BATCH, D_MODEL))
    kv_scale = jnp.repeat(jnp.array([1.0, DEEPNORM_BETA] * 3, F32), KV_W)
    col_scale = jnp.concatenate([jnp.ones((D_A,), F32), kv_scale,
                                 jnp.ones((D_IN - D_A - 6 * KV_W,), F32)])
    w_in = unif((DEPTH, D_MODEL, D_IN), D_MODEL ** -0.5) * col_scale
    w_out = unif((DEPTH, D_MIX, D_MODEL), DEEPNORM_BETA * D_MIX ** -0.5)
    cmp_pe = unif((DEPTH, 2, CMP_BLOCK, HEAD_DIM), 0.1)
    cmp_w = unif((DEPTH, 2, CMP_BLOCK * HEAD_DIM, HEAD_DIM), (CMP_BLOCK * HEAD_DIM) ** -0.5)
    conv_w = unif((DEPTH, CONV_W, D_B), CONV_W ** -0.5)
    conv_b = unif((DEPTH, D_B), 0.01)
    conv_ln_g = 1.0 + unif((DEPTH, D_B), 0.05)
    conv_ln_b = unif((DEPTH, D_B), 0.01)
    sgu_ln_g = 1.0 + unif((DEPTH, D_C), 0.05)
    sgu_ln_b = unif((DEPTH, D_C), 0.01)
    sgu_w = unif((DEPTH, N_HEADS_C, CHUNK, CHUNK), CHUNK ** -0.5)
    sgu_b = 1.0 + unif((DEPTH, N_HEADS_C, CHUNK), 0.1)
    ada_w = unif((DEPTH, D_MODEL, 6 * D_MODEL), 0.1 * D_MODEL ** -0.5)
    ada_b = unif((DEPTH, 6 * D_MODEL), 0.01)
    ln_g = 1.0 + unif((DEPTH, 2, D_MODEL), 0.05)
    ln_b = unif((DEPTH, 2, D_MODEL), 0.01)
    router_w = unif((DEPTH, D_MODEL, N_EXPERTS), D_MODEL ** -0.5)
    router_b = unif((DEPTH, N_EXPERTS), 0.01)
    moe_w1 = unif((DEPTH, N_EXPERTS, D_MODEL, 2 * D_FF), D_MODEL ** -0.5)
    moe_b1 = unif((DEPTH, N_EXPERTS, 2 * D_FF), 0.01)
    moe_w2 = unif((DEPTH, N_EXPERTS, D_FF, D_MODEL), DEEPNORM_BETA * D_FF ** -0.5)
    moe_b2 = unif((DEPTH, N_EXPERTS, D_MODEL), 0.01)
    return {
        "x_prompt": x_prompt, "x_sample": x_sample,
        "cache_kv_cmp": cache_kv_cmp, "cache_kv_slc": cache_kv_slc,
        "cache_kv_win": cache_kv_win, "state_conv": state_conv,
        "page_table": page_table, "c_prompt": c_prompt, "c_sample": c_sample,
        "w_in": w_in, "w_out": w_out, "cmp_pe": cmp_pe, "cmp_w": cmp_w,
        "conv_w": conv_w, "conv_b": conv_b, "conv_ln_g": conv_ln_g, "conv_ln_b": conv_ln_b,
        "sgu_ln_g": sgu_ln_g, "sgu_ln_b": sgu_ln_b, "sgu_w": sgu_w, "sgu_b": sgu_b,
        "ada_w": ada_w, "ada_b": ada_b, "ln_g": ln_g, "ln_b": ln_b,
        "router_w": router_w, "router_b": router_b,
        "moe_w1": moe_w1, "moe_b1": moe_b1, "moe_w2": moe_w2, "moe_b2": moe_b2,
    }


def reference(x_prompt, x_sample, cache_kv_cmp, cache_kv_slc, cache_kv_win, state_conv, page_table,
              c_prompt, c_sample, w_in, w_out, cmp_pe, cmp_w, conv_w, conv_b, conv_ln_g, conv_ln_b,
              sgu_ln_g, sgu_ln_b, sgu_w, sgu_b, ada_w, ada_b, ln_g, ln_b, router_w, router_b,
              moe_w1, moe_b1, moe_w2, moe_b2):
    xp, xs = x_prompt, x_sample
    dec_b = x_sample.shape[0]
    cmp_p, cmp_s, slc_p, slc_s, win_p, win_s, conv_p, conv_s, vch_s = ([] for _ in range(9))
    for l in range(DEPTH):
        mix_w = (w_in[l], w_out[l], cmp_pe[l], cmp_w[l], conv_w[l], conv_b[l], conv_ln_g[l], conv_ln_b[l],
                 sgu_ln_g[l], sgu_ln_b[l], sgu_w[l], sgu_b[l])
        xp, st_p = decoder_layer(xp, c_prompt, lambda hh: mixer_prompt(hh, *mix_w), l, ada_w, ada_b,
                                 ln_g, ln_b, router_w, router_b, moe_w1, moe_b1, moe_w2, moe_b2)
        past_c = cache_kv_cmp[l, page_table].reshape(dec_b, -1, 2, N_KV, HEAD_DIM)
        past_s = cache_kv_slc[l, page_table].reshape(dec_b, -1, 2, N_KV, HEAD_DIM)
        xs, st_s = decoder_layer(
            xs, c_sample,
            lambda hh: mixer_sample(hh, past_c, past_s, cache_kv_win[l], state_conv[l], *mix_w),
            l, ada_w, ada_b, ln_g, ln_b, router_w, router_b, moe_w1, moe_b1, moe_w2, moe_b2)
        cmp_p.append(st_p[0]); slc_p.append(st_p[1]); win_p.append(st_p[2]); conv_p.append(st_p[3])
        cmp_s.append(st_s[0]); slc_s.append(st_s[1]); win_s.append(st_s[2]); conv_s.append(st_s[3])
        vch_s.append(st_s[4])
    new_kv_cmp_prompt = jnp.stack(cmp_p)
    new_kv_cmp_sample = jnp.stack(cmp_s)
    new_kv_slc_prompt = jnp.stack(slc_p)
    new_kv_slc_sample = jnp.stack(slc_s)
    new_kv_win_prompt = jnp.stack(win_p)
    new_kv_win_sample = jnp.stack(win_s)
    new_conv_prompt = jnp.stack(conv_p)
    new_conv_sample = jnp.stack(conv_s)
    new_vchunk_sample = jnp.stack(vch_s)
    return (xp, xs, new_kv_cmp_prompt, new_kv_cmp_sample, new_kv_slc_prompt, new_kv_slc_sample,
            new_kv_win_prompt, new_kv_win_sample, new_conv_prompt, new_conv_sample, new_vchunk_sample)
```

```python
import functools

import jax
import jax.numpy as jnp
import numpy as np
from jax import lax
from jax.experimental import pallas as pl
from jax.experimental.pallas import tpu as pltpu

F32 = jnp.float32
BF16 = jnp.bfloat16

D_MODEL = 4096
DEPTH = 2
HEAD_DIM = 128
D_A = D_MODEL // 2
D_B = D_MODEL // 4
D_C = D_MODEL // 4
D_MIX = D_A + D_B + D_C
N_HEADS_A = D_A // HEAD_DIM
N_KV = 2
GQA = N_HEADS_A // N_KV
KV_W = N_KV * HEAD_DIM
N_BRANCH = 3
N_HEADS_C = D_C // HEAD_DIM
CMP_BLOCK = 32
CMP_STRIDE = 16
SLC_BLOCK = 64
N_SLC = 16
WINDOW = 512
SEL_QBLOCK = 64
WIN_QBLOCK = 128
FORCE_BONUS = 1.0e4
CONV_W = 31
CHUNK = 128
N_EXPERTS = 32
TOP_K = 4
D_FF = D_MODEL // 2
SWIGLU_ALPHA = 1.702
SWIGLU_LIMIT = 7.0
DEEPNORM_ALPHA = (2 * DEPTH) ** 0.25
LN_EPS = 1e-5
NEG_INF = -1e30
SPLITS = (D_A, 6 * KV_W, N_BRANCH * N_HEADS_A, 2 * D_B, 2 * D_C)
D_IN = sum(SPLITS)

LANES = 128
V7X_VMEM_BYTES = 64 * 1024 * 1024
VMEM_LIMIT = 56 * 1024 * 1024

GATE_COL = D_A + 6 * KV_W
PB_COL = GATE_COL + LANES
PC_COL = PB_COL + 2 * D_B
D_IN_PAD = 8192

MOE_TM = 256


def _cparams(sem):
    return pltpu.CompilerParams(dimension_semantics=sem, vmem_limit_bytes=VMEM_LIMIT)


def _mm_body(*refs, n_k, modulate, has_bias):
    it = iter(refs)
    x_ref = next(it)
    sc_ref = next(it) if modulate else None
    sh_ref = next(it) if modulate else None
    w_ref = next(it)
    b_ref = next(it) if has_bias else None
    o_ref = next(it)
    acc_ref = next(it)
    k = pl.program_id(2)

    @pl.when(k == 0)
    def _():
        acc_ref[...] = jnp.zeros_like(acc_ref)

    x = x_ref[...]
    if modulate:
        x = x.astype(F32) * (1.0 + sc_ref[...]) + sh_ref[...]
    acc_ref[...] += jnp.dot(x.astype(BF16), w_ref[...].astype(BF16), preferred_element_type=F32)

    @pl.when(k == n_k - 1)
    def _():
        r = acc_ref[...]
        if has_bias:
            r = r + b_ref[...]
        o_ref[...] = r.astype(o_ref.dtype)


def _matmul(x, w, *, tm, tn, tk, out_dtype=F32, bias=None, mod=None):
    m, kk = x.shape
    n = w.shape[1]
    tm, tn, tk = min(tm, m), min(tn, n), min(tk, kk)
    assert m % tm == 0 and n % tn == 0 and kk % tk == 0
    n_k = kk // tk
    in_specs = [pl.BlockSpec((tm, tk), lambda i, j, k: (i, k))]
    args = [x]
    if mod is not None:
        g, r, _ = mod[0].shape
        rows = m // g
        assert rows % tm == 0 and r in (1, tm) and (r == 1 or rows == tm)
        per = rows // tm
        spec = pl.BlockSpec((None, r, tk), lambda i, j, k: (i // per, 0, k))
        in_specs += [spec, spec]
        args += [mod[0], mod[1]]
    in_specs.append(pl.BlockSpec((tk, tn), lambda i, j, k: (k, j)))
    args.append(w)
    if bias is not None:
        in_specs.append(pl.BlockSpec((1, tn), lambda i, j, k: (0, j)))
        args.append(bias.reshape(1, n))
    return pl.pallas_call(
        functools.partial(_mm_body, n_k=n_k, modulate=mod is not None, has_bias=bias is not None),
        grid=(m // tm, n // tn, n_k),
        in_specs=in_specs,
        out_specs=pl.BlockSpec((tm, tn), lambda i, j, k: (i, j)),
        out_shape=jax.ShapeDtypeStruct((m, n), out_dtype),
        scratch_shapes=[pltpu.VMEM((tm, tn), F32)],
        compiler_params=_cparams(("parallel", "parallel", "arbitrary")),
        name="matmul",
    )(*args)


def _dn_body(*refs, modulate):
    it = iter(refs)
    x_ref, m_ref, gate_ref, g_ref, b_ref = (next(it) for _ in range(5))
    sc_ref = next(it) if modulate else None
    sh_ref = next(it) if modulate else None
    y_ref = next(it)
    ym_ref = next(it) if modulate else None
    z = DEEPNORM_ALPHA * x_ref[...] + (1.0 + gate_ref[...]) * m_ref[...]
    mu = jnp.mean(z, axis=-1, keepdims=True)
    zc = z - mu
    var = jnp.mean(zc * zc, axis=-1, keepdims=True)
    y = zc * lax.rsqrt(var + LN_EPS) * g_ref[...] + b_ref[...]
    y_ref[...] = y
    if modulate:
        ym_ref[...] = (y * (1.0 + sc_ref[...]) + sh_ref[...]).astype(ym_ref.dtype)


def _deepnorm(x, m, gate, ln_g, ln_b, mod=None, *, tm=256):
    n, d = x.shape
    tm = min(tm, n)
    assert n % tm == 0
    g, r, _ = gate.shape
    rows = n // g
    assert rows % tm == 0 and r in (1, tm) and (r == 1 or rows == tm)
    per = rows // tm
    row = pl.BlockSpec((tm, d), lambda i: (i, 0))
    grp = pl.BlockSpec((None, r, d), lambda i: (i // per, 0, 0))
    vec = pl.BlockSpec((1, d), lambda i: (0, 0))
    in_specs = [row, row, grp, vec, vec]
    args = [x, m, gate, ln_g.reshape(1, d), ln_b.reshape(1, d)]
    out_specs = [row]
    out_shape = [jax.ShapeDtypeStruct((n, d), F32)]
    if mod is not None:
        in_specs += [grp, grp]
        args += [mod[0], mod[1]]
        out_specs.append(row)
        out_shape.append(jax.ShapeDtypeStruct((n, d), BF16))
    out = pl.pallas_call(
        functools.partial(_dn_body, modulate=mod is not None),
        grid=(n // tm,),
        in_specs=in_specs,
        out_specs=out_specs,
        out_shape=out_shape,
        compiler_params=_cparams(("parallel",)),
        name="deepnorm",
    )(*args)
    return out if mod is not None else out[0]


def _gmm1_body(be_ref, nu_ref, x_ref, wg_ref, wl_ref, bg_ref, bl_ref, o_ref):
    i = pl.program_id(1)

    @pl.when(i < nu_ref[0])
    def _():
        x = x_ref[...]
        hg = jnp.dot(x, wg_ref[...].astype(BF16), preferred_element_type=F32) + bg_ref[...]
        hl = jnp.dot(x, wl_ref[...].astype(BF16), preferred_element_type=F32) + bl_ref[...]
        hg = jnp.minimum(hg, SWIGLU_LIMIT)
        hl = jnp.clip(hl, -SWIGLU_LIMIT, SWIGLU_LIMIT)
        act = hg * jax.nn.sigmoid(SWIGLU_ALPHA * hg) * (hl + 1.0)
        o_ref[...] = act.astype(o_ref.dtype)


def _gmm2_body(be_ref, nu_ref, a_ref, w_ref, b_ref, rw_ref, o_ref):
    i = pl.program_id(1)

    @pl.when(i < nu_ref[0])
    def _():
        y = jnp.dot(a_ref[...], w_ref[...].astype(BF16), preferred_element_type=F32) + b_ref[...]
        o_ref[...] = y * rw_ref[...]


def _moe_experts(xs, row_w, blk_exp, n_used, layer, w1, b1, w2, b2, *, tn1=256, tn2=1024):
    cap, d = xs.shape
    tm = MOE_TM
    n_blk = cap // tm
    nj1 = D_FF // tn1
    nj2 = d // tn2
    b1r = b1.reshape(DEPTH, N_EXPERTS, 1, 2 * D_FF)
    b2r = b2.reshape(DEPTH, N_EXPERTS, 1, d)

    def row_blk(i, nu):
        return jnp.minimum(i, nu[0] - 1)

    act = pl.pallas_call(
        _gmm1_body,
        grid_spec=pltpu.PrefetchScalarGridSpec(
            num_scalar_prefetch=2,
            grid=(nj1, n_blk),
            in_specs=[
                pl.BlockSpec((tm, d), lambda j, i, be, nu: (row_blk(i, nu), 0)),
                pl.BlockSpec((None, None, d, tn1), lambda j, i, be, nu: (layer, be[i], 0, j)),
                pl.BlockSpec((None, None, d, tn1), lambda j, i, be, nu: (layer, be[i], 0, nj1 + j)),
                pl.BlockSpec((None, None, 1, tn1), lambda j, i, be, nu: (layer, be[i], 0, j)),
                pl.BlockSpec((None, None, 1, tn1), lambda j, i, be, nu: (layer, be[i], 0, nj1 + j)),
            ],
            out_specs=pl.BlockSpec((tm, tn1), lambda j, i, be, nu: (row_blk(i, nu), j)),
        ),
        out_shape=jax.ShapeDtypeStruct((cap, D_FF), BF16),
        compiler_params=_cparams(("arbitrary", "arbitrary")),
        name="moe_gmm1",
    )(blk_exp, n_used, xs, w1, w1, b1r, b1r)

    ys = pl.pallas_call(
        _gmm2_body,
        grid_spec=pltpu.PrefetchScalarGridSpec(
            num_scalar_prefetch=2,
            grid=(nj2, n_blk),
            in_specs=[
                pl.BlockSpec((tm, D_FF), lambda j, i, be, nu: (row_blk(i, nu), 0)),
                pl.BlockSpec((None, None, D_FF, tn2), lambda j, i, be, nu: (layer, be[i], 0, j)),
                pl.BlockSpec((None, None, 1, tn2), lambda j, i, be, nu: (layer, be[i], 0, j)),
                pl.BlockSpec((tm, 1), lambda j, i, be, nu: (row_blk(i, nu), 0)),
            ],
            out_specs=pl.BlockSpec((tm, tn2), lambda j, i, be, nu: (row_blk(i, nu), j)),
        ),
        out_shape=jax.ShapeDtypeStruct((cap, d), F32),
        compiler_params=_cparams(("arbitrary", "arbitrary")),
        name="moe_gmm2",
    )(blk_exp, n_used, act, w2, b2r, row_w)
    return ys


def _route(logits):
    n = logits.shape[0]
    tm = MOE_TM
    cap = -(-(n * TOP_K + N_EXPERTS * (tm - 1)) // tm) * tm
    n_blk = cap // tm
    top_val, top_exp = lax.top_k(logits, TOP_K)
    weights = jax.nn.softmax(top_val, axis=-1)
    onehot = (top_exp[:, :, None] == jnp.arange(N_EXPERTS, dtype=jnp.int32)).any(axis=1).astype(jnp.int32)
    counts = onehot.sum(axis=0)
    padded = (counts + tm - 1) // tm * tm
    pad_end = jnp.cumsum(padded)
    pad_start = pad_end - padded
    rank = jnp.cumsum(onehot, axis=0) - onehot
    pos = pad_start[top_exp] + jnp.take_along_axis(rank, top_exp, axis=1)
    tok = jnp.broadcast_to(jnp.arange(n, dtype=jnp.int32)[:, None], (n, TOP_K))
    row_tok = jnp.zeros((cap,), jnp.int32).at[pos.reshape(-1)].set(tok.reshape(-1))
    row_w = jnp.zeros((cap,), F32).at[pos.reshape(-1)].set(weights.reshape(-1))
    blk_exp = jnp.minimum(jnp.searchsorted(pad_end, jnp.arange(n_blk, dtype=jnp.int32) * tm, side="right"),
                          N_EXPERTS - 1).astype(jnp.int32)
    n_used = (pad_end[-1:] // tm).astype(jnp.int32)
    return row_tok, row_w.reshape(cap, 1), blk_exp, n_used, pos


def _moe(xm, layer, router_w, router_b, w1, b1, w2, b2):
    n, d = xm.shape
    rw = jnp.pad(router_w, ((0, 0), (0, LANES - N_EXPERTS))).astype(BF16)
    rb = jnp.pad(router_b, (0, LANES - N_EXPERTS))
    n_pad = -(-n // 256) * 256
    logits = _matmul(jnp.pad(xm, ((0, n_pad - n), (0, 0))), rw, tm=256, tn=LANES, tk=d, bias=rb)[:n, :N_EXPERTS]
    row_tok, row_w, blk_exp, n_used, pos = _route(logits)
    xs = jnp.take(xm, row_tok, axis=0)
    ys = _moe_experts(xs, row_w, blk_exp, n_used, layer, w1, b1, w2, b2)
    f = jnp.take(ys, pos.reshape(-1), axis=0).reshape(n, TOP_K, d)
    return f[:, 0] + f[:, 1] + f[:, 2] + f[:, 3]


def _layer_norm(x, g, b):
    mu = jnp.mean(x, axis=-1, keepdims=True)
    xc = x - mu
    var = jnp.mean(xc * xc, axis=-1, keepdims=True)
    return xc * lax.rsqrt(var + LN_EPS) * g + b


def _masked_softmax(s, mask):
    s = jnp.where(mask, s, NEG_INF)
    m = jnp.max(s, axis=-1, keepdims=True)
    p = jnp.exp(s - m) * mask
    return p / jnp.maximum(jnp.sum(p, axis=-1, keepdims=True), 1e-30)


def _pad_rows(x, mult):
    extra = (-x.shape[1]) % mult
    return jnp.pad(x, [(0, 0), (0, extra)] + [(0, 0)] * (x.ndim - 2))


def _split_proj(p):
    b_, t, _ = p.shape
    q = p[..., :D_A].reshape(b_, t, N_HEADS_A, HEAD_DIM)
    kv = p[..., D_A:GATE_COL].reshape(b_, t, 6, N_KV, HEAD_DIM)
    gates = jax.nn.sigmoid(p[..., GATE_COL:GATE_COL + N_BRANCH * N_HEADS_A]).reshape(b_, t, N_BRANCH, N_HEADS_A)
    pb = p[..., PB_COL:PC_COL]
    pc = p[..., PC_COL:PC_COL + 2 * D_C]
    return q, kv, gates, pb, pc


def _compress(k, pe, w):
    b_, length, kvh, hd = k.shape
    nch = length // CMP_STRIDE
    r = CMP_BLOCK // CMP_STRIDE
    nc = nch - r + 1
    ch = k.reshape(b_, nch, CMP_STRIDE, kvh, hd)
    blocks = jnp.concatenate([ch[:, s:s + nc] for s in range(r)], axis=2)
    blocks = blocks + pe[:, None, :]
    flat = jnp.swapaxes(blocks, 2, 3).reshape(b_, nc, kvh, CMP_BLOCK * hd)
    return jnp.dot(flat, w)


def _nsa_cmp_slc(q, q_pos, kc, vc, ks, vs):
    b_, t, h, hd = q.shape
    nc = kc.shape[1]
    ns = ks.shape[1] // SLC_BLOCK
    n_sel = min(N_SLC, ns)
    scale = hd ** -0.5
    ksb = ks.reshape(b_, ns, SLC_BLOCK, N_KV, hd).transpose(0, 3, 1, 2, 4)
    vsb = vs.reshape(b_, ns, SLC_BLOCK, N_KV, hd).transpose(0, 3, 1, 2, 4)
    c_start = jnp.arange(nc, dtype=jnp.int32) * CMP_STRIDE
    c_end = c_start + CMP_BLOCK - 1
    s_start = jnp.arange(ns, dtype=jnp.int32) * SLC_BLOCK
    cover = ((c_start[:, None] < s_start[None, :] + SLC_BLOCK)
             & (c_start[:, None] + CMP_BLOCK > s_start[None, :])).astype(F32)
    bi = jnp.arange(b_)[:, None, None, None]
    gi = jnp.arange(N_KV)[None, :, None, None]
    blk_ids = jnp.arange(ns, dtype=jnp.int32)

    def one_block(args):
        qb, pos = args
        nq = pos.shape[0]
        qg = qb.reshape(b_, nq, N_KV, GQA, hd)
        s_c = jnp.einsum("bqkgd,bnkd->bkgqn", qg, kc) * scale
        p_c = _masked_softmax(s_c, c_end[None, :] <= pos[:, None])
        o_c = jnp.einsum("bkgqn,bnkd->bqkgd", p_c, vc)
        imp = jnp.einsum("bkgqn,ns->bkqs", p_c, cover)
        cur = pos // SLC_BLOCK
        valid = blk_ids[None, :] <= cur[:, None]
        forced = (blk_ids[None, :] == 0) | (blk_ids[None, :] == cur[:, None]) | (blk_ids[None, :] == cur[:, None] - 1)
        score = jnp.where(valid, imp + jnp.where(forced, FORCE_BONUS, 0.0), NEG_INF)
        _, idx = lax.top_k(score, n_sel)
        kg = ksb[bi, gi, idx].reshape(b_, N_KV, nq, n_sel * SLC_BLOCK, hd)
        vg = vsb[bi, gi, idx].reshape(b_, N_KV, nq, n_sel * SLC_BLOCK, hd)
        kpos = idx[..., None] * SLC_BLOCK + jnp.arange(SLC_BLOCK, dtype=jnp.int32)
        m_s = (kpos <= pos[None, None, :, None, None]).reshape(b_, N_KV, 1, nq, n_sel * SLC_BLOCK)
        s_s = jnp.einsum("bqkgd,bkqmd->bkgqm", qg, kg) * scale
        p_s = _masked_softmax(s_s, m_s)
        o_s = jnp.einsum("bkgqm,bkqmd->bqkgd", p_s, vg)
        return o_c.reshape(b_, nq, h, hd), o_s.reshape(b_, nq, h, hd)

    qblk = SEL_QBLOCK if t % SEL_QBLOCK == 0 else t
    nblk = t // qblk
    qs = jnp.swapaxes(q.reshape(b_, nblk, qblk, h, hd), 0, 1)
    o_c, o_s = lax.map(one_block, (qs, q_pos.reshape(nblk, qblk)))
    back = lambda o: jnp.swapaxes(o, 0, 1).reshape(b_, t, h, hd)
    return back(o_c), back(o_s)


def _window_mask(tq, tk):
    return (tk <= tq) & (tk > tq - WINDOW) & (tk >= 0)


def _window_attn_banded(q, k, v):
    b_, t, h, hd = q.shape
    nb = t // WIN_QBLOCK
    n_prev = -(-WINDOW // WIN_QBLOCK)

    def bands(x):
        xb = jnp.pad(x.reshape(b_, nb, WIN_QBLOCK, N_KV, hd), ((0, 0), (n_prev, 0), (0, 0), (0, 0), (0, 0)))
        return jnp.concatenate([xb[:, s:s + nb] for s in range(n_prev + 1)], axis=2)

    kw, vw = bands(k), bands(v)
    qpos = jnp.arange(t, dtype=jnp.int32).reshape(nb, WIN_QBLOCK)
    kpos = ((jnp.arange(nb, dtype=jnp.int32)[:, None] - n_prev) * WIN_QBLOCK
            + jnp.arange((n_prev + 1) * WIN_QBLOCK, dtype=jnp.int32)[None, :])
    mask = _window_mask(qpos[:, :, None], kpos[:, None, :])
    qg = q.reshape(b_, nb, WIN_QBLOCK, N_KV, GQA, hd)
    s = jnp.einsum("bnqkgd,bnmkd->bnkgqm", qg, kw) * hd ** -0.5
    p = _masked_softmax(s, mask[None, :, None, None])
    o = jnp.einsum("bnkgqm,bnmkd->bnqkgd", p, vw)
    return o.reshape(b_, t, h, hd)


def _window_attn_direct(q, q_pos, k, v, k_pos):
    b_, t, h, hd = q.shape
    qg = q.reshape(b_, t, N_KV, GQA, hd)
    s = jnp.einsum("bqkgd,bmkd->bkgqm", qg, k) * hd ** -0.5
    p = _masked_softmax(s, _window_mask(q_pos[:, None], k_pos[None, :]))
    o = jnp.einsum("bkgqm,bmkd->bqkgd", p, v)
    return o.reshape(b_, t, h, hd)


def _nsa_merge(gates, o_cmp, o_slc, o_win):
    o = (gates[:, :, 0, :, None] * o_cmp + gates[:, :, 1, :, None] * o_slc
         + gates[:, :, 2, :, None] * o_win)
    return o.reshape(o.shape[0], o.shape[1], D_A)


def _glu_in(pb):
    a, g = jnp.split(pb, 2, axis=-1)
    return a * jax.nn.sigmoid(g)


def _conformer_conv(glu_ext, conv_w, conv_b, ln_g, ln_b):
    y = lax.conv_general_dilated(glu_ext, conv_w[:, None, :], window_strides=(1,),
                                 padding="VALID", dimension_numbers=("NWC", "WIO", "NWC"),
                                 feature_group_count=D_B)
    return jax.nn.silu(_layer_norm(y + conv_b, ln_g, ln_b))


def _sgu_split(pc, ln_g, ln_b):
    u, v = jnp.split(jax.nn.gelu(pc), 2, axis=-1)
    return u, _layer_norm(v, ln_g, ln_b)


def _chunk_mix(v, w_s, b_s):
    b_, length, _ = v.shape
    vh = v.reshape(b_, length // CHUNK, CHUNK, N_HEADS_C, HEAD_DIM)
    wm = w_s * jnp.tril(jnp.ones((CHUNK, CHUNK), w_s.dtype))
    out = jnp.einsum("hts,bnshd->bnthd", wm, vh) + jnp.swapaxes(b_s, 0, 1)[None, None, :, :, None]
    return out.reshape(b_, length, D_C)


def _mixer_prompt(p, cmp_pe, cmp_w, conv_w, conv_b, conv_ln_g, conv_ln_b, sgu_ln_g, sgu_ln_b, sgu_w, sgu_b):
    b_, t, _ = p.shape
    q, kv, gates, pb, pc = _split_proj(p)
    pos = jnp.arange(t, dtype=jnp.int32)
    kc = _compress(kv[:, :, 0], cmp_pe[0], cmp_w[0])
    vc = _compress(kv[:, :, 1], cmp_pe[1], cmp_w[1])
    o_cmp, o_slc = _nsa_cmp_slc(q, pos, kc, vc, kv[:, :, 2], kv[:, :, 3])
    o_win = _window_attn_banded(q, kv[:, :, 4], kv[:, :, 5])
    o_a = _nsa_merge(gates, o_cmp, o_slc, o_win)
    glu = _glu_in(pb)
    o_b = _conformer_conv(jnp.pad(glu, ((0, 0), (CONV_W - 1, 0), (0, 0))), conv_w, conv_b, conv_ln_g, conv_ln_b)
    u, v = _sgu_split(pc, sgu_ln_g, sgu_ln_b)
    o_c = u * _chunk_mix(v, sgu_w, sgu_b)
    cat = jnp.concatenate([o_a, o_b, o_c], axis=-1).astype(BF16)
    w_keep = min(WINDOW, t)
    new_state = (kv[:, :, 0:2], kv[:, :, 2:4], kv[:, t - w_keep:, 4:6], glu[:, t - (CONV_W - 1):])
    return cat, new_state


def _mixer_sample(p, past_cmp, past_slc, win_buf, conv_buf, cmp_pe, cmp_w, conv_w, conv_b,
                  conv_ln_g, conv_ln_b, sgu_ln_g, sgu_ln_b, sgu_w, sgu_b):
    b_, t, _ = p.shape
    n_past = past_cmp.shape[1]
    w_buf = win_buf.shape[1]
    q, kv, gates, pb, pc = _split_proj(p)
    pos = n_past + jnp.arange(t, dtype=jnp.int32)
    full_c = _pad_rows(jnp.concatenate([past_cmp, kv[:, :, 0:2]], axis=1), CMP_STRIDE)
    kc = _compress(full_c[:, :, 0], cmp_pe[0], cmp_w[0])
    vc = _compress(full_c[:, :, 1], cmp_pe[1], cmp_w[1])
    full_s = _pad_rows(jnp.concatenate([past_slc, kv[:, :, 2:4]], axis=1), SLC_BLOCK)
    o_cmp, o_slc = _nsa_cmp_slc(q, pos, kc, vc, full_s[:, :, 0], full_s[:, :, 1])
    win = jnp.concatenate([win_buf, kv[:, :, 4:6]], axis=1)
    kpos = n_past - w_buf + jnp.arange(w_buf + t, dtype=jnp.int32)
    o_win = _window_attn_direct(q, pos, win[:, :, 0], win[:, :, 1], kpos)
    o_a = _nsa_merge(gates, o_cmp, o_slc, o_win)
    glu = _glu_in(pb)
    glu_ext = jnp.concatenate([conv_buf, glu], axis=1)
    o_b = _conformer_conv(glu_ext, conv_w, conv_b, conv_ln_g, conv_ln_b)
    u, v = _sgu_split(pc, sgu_ln_g, sgu_ln_b)
    o_c = u * _chunk_mix(_pad_rows(v, CHUNK), sgu_w, sgu_b)[:, :t]
    cat = jnp.concatenate([o_a, o_b, o_c], axis=-1).astype(BF16)
    new_state = (kv[:, :, 0:2], kv[:, :, 2:4], win[:, t:], glu_ext[:, t:], v)
    return cat, new_state


def _pad_w_in(w):
    k = w.shape[0]
    g = N_BRANCH * N_HEADS_A
    return jnp.concatenate([
        w[:, :GATE_COL], w[:, GATE_COL:GATE_COL + g], jnp.zeros((k, LANES - g), w.dtype),
        w[:, GATE_COL + g:], jnp.zeros((k, D_IN_PAD - PC_COL - 2 * D_C), w.dtype)], axis=1).astype(BF16)


def kernel(x_prompt, x_sample, cache_kv_cmp, cache_kv_slc, cache_kv_win, state_conv, page_table,
           c_prompt, c_sample, w_in, w_out, cmp_pe, cmp_w, conv_w, conv_b, conv_ln_g, conv_ln_b,
           sgu_ln_g, sgu_ln_b, sgu_w, sgu_b, ada_w, ada_b, ln_g, ln_b, router_w, router_b,
           moe_w1, moe_b1, moe_w2, moe_b2):
    bp, tp, d = x_prompt.shape
    bs, ts, _ = x_sample.shape
    n_p, n_s = bp * tp, bs * ts
    xp = x_prompt.reshape(n_p, d)
    xs = x_sample.reshape(n_s, d)
    c_all = jnp.concatenate([c_prompt, c_sample], axis=0)
    c_all = jax.nn.silu(jnp.pad(c_all, ((0, 16 - bp - bs), (0, 0))))
    st = [[] for _ in range(9)]
    for l in range(DEPTH):
        mod = _matmul(c_all, ada_w[l], tm=16, tn=2048, tk=1024, bias=ada_b[l])
        mod_p = mod[:bp].reshape(bp, 1, 6, d)
        mod_s = jnp.repeat(mod[bp:bp + bs], ts, axis=0).reshape(1, n_s, 6, d)
        mp = [mod_p[:, :, i] for i in range(6)]
        ms = [mod_s[:, :, i] for i in range(6)]
        w_in_l = _pad_w_in(w_in[l])
        w_out_l = w_out[l].astype(BF16)
        mix_w = (cmp_pe[l], cmp_w[l], conv_w[l], conv_b[l], conv_ln_g[l], conv_ln_b[l],
                 sgu_ln_g[l], sgu_ln_b[l], sgu_w[l], sgu_b[l])

        pp = _matmul(xp, w_in_l, tm=1024, tn=1024, tk=512, mod=(mp[1], mp[0]))
        ps = _matmul(xs, w_in_l, tm=n_s, tn=1024, tk=512, mod=(ms[1], ms[0]))
        cat_p, st_p = _mixer_prompt(pp.reshape(bp, tp, D_IN_PAD), *mix_w)
        past_c = cache_kv_cmp[l, page_table].reshape(bs, -1, 2, N_KV, HEAD_DIM)
        past_s = cache_kv_slc[l, page_table].reshape(bs, -1, 2, N_KV, HEAD_DIM)
        cat_s, st_s = _mixer_sample(ps.reshape(bs, ts, D_IN_PAD), past_c, past_s, cache_kv_win[l], state_conv[l], *mix_w)
        m_p = _matmul(cat_p.reshape(n_p, D_MIX), w_out_l, tm=1024, tn=1024, tk=512)
        m_s = _matmul(cat_s.reshape(n_s, D_MIX), w_out_l, tm=n_s, tn=1024, tk=512)
        xp, xmp = _deepnorm(xp, m_p, mp[2], ln_g[l, 0], ln_b[l, 0], mod=(mp[4], mp[3]))
        xs, xms = _deepnorm(xs, m_s, ms[2], ln_g[l, 0], ln_b[l, 0], mod=(ms[4], ms[3]), tm=n_s)
        f = _moe(jnp.concatenate([xmp, xms], axis=0), l, router_w[l], router_b[l], moe_w1, moe_b1, moe_w2, moe_b2)
        xp = _deepnorm(xp, f[:n_p], mp[5], ln_g[l, 1], ln_b[l, 1])
        xs = _deepnorm(xs, f[n_p:], ms[5], ln_g[l, 1], ln_b[l, 1], tm=n_s)
        for i in range(4):
            st[2 * i].append(st_p[i])
            st[2 * i + 1].append(st_s[i])
        st[8].append(st_s[4])
    return (xp.reshape(bp, tp, d), xs.reshape(bs, ts, d)) + tuple(jnp.stack(s) for s in st)
```

```python
import functools

import jax
import jax.numpy as jnp
import numpy as np
from jax import lax
from jax.experimental import pallas as pl
from jax.experimental.pallas import tpu as pltpu

F32 = jnp.float32
BF16 = jnp.bfloat16

D_MODEL = 4096
DEPTH = 2
HEAD_DIM = 128
D_A = D_MODEL // 2
D_B = D_MODEL // 4
D_C = D_MODEL // 4
D_MIX = D_A + D_B + D_C
N_HEADS_A = D_A // HEAD_DIM
N_KV = 2
GQA = N_HEADS_A // N_KV
KV_W = N_KV * HEAD_DIM
N_BRANCH = 3
N_HEADS_C = D_C // HEAD_DIM
CMP_BLOCK = 32
CMP_STRIDE = 16
SLC_BLOCK = 64
N_SLC = 16
WINDOW = 512
SEL_QBLOCK = 64
WIN_QBLOCK = 128
FORCE_BONUS = 1.0e4
CONV_W = 31
CHUNK = 128
N_EXPERTS = 32
TOP_K = 4
D_FF = D_MODEL // 2
SWIGLU_ALPHA = 1.702
SWIGLU_LIMIT = 7.0
DEEPNORM_ALPHA = (2 * DEPTH) ** 0.25
LN_EPS = 1e-5
NEG_INF = -1e30
SPLITS = (D_A, 6 * KV_W, N_BRANCH * N_HEADS_A, 2 * D_B, 2 * D_C)
D_IN = sum(SPLITS)

LANES = 128
V7X_VMEM_BYTES = 64 * 1024 * 1024
VMEM_LIMIT = 56 * 1024 * 1024

KV_COL = D_A
GATE_COL = D_A + 6 * KV_W
PB_COL = D_MODEL
PC_COL = PB_COL + 2 * D_B
D_IN_PAD = PC_COL + 2 * D_C

MOE_TM = 256


def _cparams(sem):
    return pltpu.CompilerParams(dimension_semantics=sem, vmem_limit_bytes=VMEM_LIMIT)


def _mm_body(*refs, n_k, modulate, has_bias):
    it = iter(refs)
    x_ref = next(it)
    sc_ref = next(it) if modulate else None
    sh_ref = next(it) if modulate else None
    w_ref = next(it)
    b_ref = next(it) if has_bias else None
    o_ref = next(it)
    acc_ref = next(it)
    k = pl.program_id(2)

    @pl.when(k == 0)
    def _():
        acc_ref[...] = jnp.zeros_like(acc_ref)

    x = x_ref[...]
    if modulate:
        x = x.astype(F32) * (1.0 + sc_ref[...]) + sh_ref[...]
    acc_ref[...] += jnp.dot(x.astype(BF16), w_ref[...].astype(BF16), preferred_element_type=F32)

    @pl.when(k == n_k - 1)
    def _():
        r = acc_ref[...]
        if has_bias:
            r = r + b_ref[...]
        o_ref[...] = r.astype(o_ref.dtype)


def _matmul(x, w, *, tm, tn, tk, out_dtype=F32, bias=None, mod=None):
    m, kk = x.shape
    n = w.shape[1]
    tm, tn, tk = min(tm, m), min(tn, n), min(tk, kk)
    assert m % tm == 0 and n % tn == 0 and kk % tk == 0
    n_k = kk // tk
    in_specs = [pl.BlockSpec((tm, tk), lambda i, j, k: (i, k))]
    args = [x]
    if mod is not None:
        g, r, _ = mod[0].shape
        rows = m // g
        assert rows % tm == 0 and r in (1, tm) and (r == 1 or rows == tm)
        per = rows // tm
        spec = pl.BlockSpec((None, r, tk), lambda i, j, k: (i // per, 0, k))
        in_specs += [spec, spec]
        args += [mod[0], mod[1]]
    in_specs.append(pl.BlockSpec((tk, tn), lambda i, j, k: (k, j)))
    args.append(w)
    if bias is not None:
        in_specs.append(pl.BlockSpec((1, tn), lambda i, j, k: (0, j)))
        args.append(bias.reshape(1, n))
    return pl.pallas_call(
        functools.partial(_mm_body, n_k=n_k, modulate=mod is not None, has_bias=bias is not None),
        grid=(m // tm, n // tn, n_k),
        in_specs=in_specs,
        out_specs=pl.BlockSpec((tm, tn), lambda i, j, k: (i, j)),
        out_shape=jax.ShapeDtypeStruct((m, n), out_dtype),
        scratch_shapes=[pltpu.VMEM((tm, tn), F32)],
        compiler_params=_cparams(("parallel", "parallel", "arbitrary")),
        name="matmul",
    )(*args)


def _dn_body(*refs, modulate):
    it = iter(refs)
    x_ref, m_ref, gate_ref, g_ref, b_ref = (next(it) for _ in range(5))
    sc_ref = next(it) if modulate else None
    sh_ref = next(it) if modulate else None
    y_ref = next(it)
    ym_ref = next(it) if modulate else None
    z = DEEPNORM_ALPHA * x_ref[...] + (1.0 + gate_ref[...]) * m_ref[...]
    mu = jnp.mean(z, axis=-1, keepdims=True)
    zc = z - mu
    var = jnp.mean(zc * zc, axis=-1, keepdims=True)
    y = zc * lax.rsqrt(var + LN_EPS) * g_ref[...] + b_ref[...]
    y_ref[...] = y
    if modulate:
        ym_ref[...] = (y * (1.0 + sc_ref[...]) + sh_ref[...]).astype(ym_ref.dtype)


def _deepnorm(x, m, gate, ln_g, ln_b, mod=None, *, tm=256):
    n, d = x.shape
    tm = min(tm, n)
    assert n % tm == 0
    g, r, _ = gate.shape
    rows = n // g
    assert rows % tm == 0 and r in (1, tm) and (r == 1 or rows == tm)
    per = rows // tm
    row = pl.BlockSpec((tm, d), lambda i: (i, 0))
    grp = pl.BlockSpec((None, r, d), lambda i: (i // per, 0, 0))
    vec = pl.BlockSpec((1, d), lambda i: (0, 0))
    in_specs = [row, row, grp, vec, vec]
    args = [x, m, gate, ln_g.reshape(1, d), ln_b.reshape(1, d)]
    out_specs = [row]
    out_shape = [jax.ShapeDtypeStruct((n, d), F32)]
    if mod is not None:
        in_specs += [grp, grp]
        args += [mod[0], mod[1]]
        out_specs.append(row)
        out_shape.append(jax.ShapeDtypeStruct((n, d), BF16))
    out = pl.pallas_call(
        functools.partial(_dn_body, modulate=mod is not None),
        grid=(n // tm,),
        in_specs=in_specs,
        out_specs=out_specs,
        out_shape=out_shape,
        compiler_params=_cparams(("parallel",)),
        name="deepnorm",
    )(*args)
    return out if mod is not None else out[0]


def _gmm1_body(be_ref, nu_ref, x_ref, wg_ref, wl_ref, bg_ref, bl_ref, o_ref):
    i = pl.program_id(1)

    @pl.when(i < nu_ref[0])
    def _():
        x = x_ref[...]
        hg = jnp.dot(x, wg_ref[...].astype(BF16), preferred_element_type=F32) + bg_ref[...]
        hl = jnp.dot(x, wl_ref[...].astype(BF16), preferred_element_type=F32) + bl_ref[...]
        hg = jnp.minimum(hg, SWIGLU_LIMIT)
        hl = jnp.clip(hl, -SWIGLU_LIMIT, SWIGLU_LIMIT)
        act = hg * jax.nn.sigmoid(SWIGLU_ALPHA * hg) * (hl + 1.0)
        o_ref[...] = act.astype(o_ref.dtype)


def _gmm2_body(be_ref, nu_ref, a_ref, w_ref, b_ref, rw_ref, o_ref):
    i = pl.program_id(1)

    @pl.when(i < nu_ref[0])
    def _():
        y = jnp.dot(a_ref[...], w_ref[...].astype(BF16), preferred_element_type=F32) + b_ref[...]
        o_ref[...] = y * rw_ref[...]


def _moe_experts(xs, row_w, blk_exp, n_used, layer, w1, b1, w2, b2, *, tn1=256, tn2=1024):
    cap, d = xs.shape
    tm = MOE_TM
    n_blk = cap // tm
    nj1 = D_FF // tn1
    nj2 = d // tn2
    b1r = b1.reshape(DEPTH, N_EXPERTS, 1, 2 * D_FF)
    b2r = b2.reshape(DEPTH, N_EXPERTS, 1, d)

    def row_blk(i, nu):
        return jnp.minimum(i, nu[0] - 1)

    act = pl.pallas_call(
        _gmm1_body,
        grid_spec=pltpu.PrefetchScalarGridSpec(
            num_scalar_prefetch=2,
            grid=(nj1, n_blk),
            in_specs=[
                pl.BlockSpec((tm, d), lambda j, i, be, nu: (row_blk(i, nu), 0)),
                pl.BlockSpec((None, None, d, tn1), lambda j, i, be, nu: (layer, be[i], 0, j)),
                pl.BlockSpec((None, None, d, tn1), lambda j, i, be, nu: (layer, be[i], 0, nj1 + j)),
                pl.BlockSpec((None, None, 1, tn1), lambda j, i, be, nu: (layer, be[i], 0, j)),
                pl.BlockSpec((None, None, 1, tn1), lambda j, i, be, nu: (layer, be[i], 0, nj1 + j)),
            ],
            out_specs=pl.BlockSpec((tm, tn1), lambda j, i, be, nu: (row_blk(i, nu), j)),
        ),
        out_shape=jax.ShapeDtypeStruct((cap, D_FF), BF16),
        compiler_params=_cparams(("arbitrary", "arbitrary")),
        name="moe_gmm1",
    )(blk_exp, n_used, xs, w1, w1, b1r, b1r)

    ys = pl.pallas_call(
        _gmm2_body,
        grid_spec=pltpu.PrefetchScalarGridSpec(
            num_scalar_prefetch=2,
            grid=(nj2, n_blk),
            in_specs=[
                pl.BlockSpec((tm, D_FF), lambda j, i, be, nu: (row_blk(i, nu), 0)),
                pl.BlockSpec((None, None, D_FF, tn2), lambda j, i, be, nu: (layer, be[i], 0, j)),
                pl.BlockSpec((None, None, 1, tn2), lambda j, i, be, nu: (layer, be[i], 0, j)),
                pl.BlockSpec((tm, 1), lambda j, i, be, nu: (row_blk(i, nu), 0)),
            ],
            out_specs=pl.BlockSpec((tm, tn2), lambda j, i, be, nu: (row_blk(i, nu), j)),
        ),
        out_shape=jax.ShapeDtypeStruct((cap, d), F32),
        compiler_params=_cparams(("arbitrary", "arbitrary")),
        name="moe_gmm2",
    )(blk_exp, n_used, act, w2, b2r, row_w)
    return ys


def _route(logits):
    n = logits.shape[0]
    tm = MOE_TM
    cap = -(-(n * TOP_K + N_EXPERTS * (tm - 1)) // tm) * tm
    n_blk = cap // tm
    top_val, top_exp = lax.top_k(logits, TOP_K)
    weights = jax.nn.softmax(top_val, axis=-1)
    onehot = (top_exp[:, :, None] == jnp.arange(N_EXPERTS, dtype=jnp.int32)).any(axis=1).astype(jnp.int32)
    counts = onehot.sum(axis=0)
    padded = (counts + tm - 1) // tm * tm
    pad_end = jnp.cumsum(padded)
    pad_start = pad_end - padded
    rank = jnp.cumsum(onehot, axis=0) - onehot
    pos = pad_start[top_exp] + jnp.take_along_axis(rank, top_exp, axis=1)
    tok = jnp.broadcast_to(jnp.arange(n, dtype=jnp.int32)[:, None], (n, TOP_K))
    row_tok = jnp.zeros((cap,), jnp.int32).at[pos.reshape(-1)].set(tok.reshape(-1))
    row_w = jnp.zeros((cap,), F32).at[pos.reshape(-1)].set(weights.reshape(-1))
    blk_exp = jnp.minimum(jnp.searchsorted(pad_end, jnp.arange(n_blk, dtype=jnp.int32) * tm, side="right"),
                          N_EXPERTS - 1).astype(jnp.int32)
    n_used = (pad_end[-1:] // tm).astype(jnp.int32)
    return row_tok, row_w.reshape(cap, 1), blk_exp, n_used, pos


def _moe(xm, layer, router_w, router_b, w1, b1, w2, b2):
    n, d = xm.shape
    rw = jnp.pad(router_w, ((0, 0), (0, LANES - N_EXPERTS))).astype(BF16)
    rb = jnp.pad(router_b, (0, LANES - N_EXPERTS))
    n_pad = -(-n // 256) * 256
    logits = _matmul(jnp.pad(xm, ((0, n_pad - n), (0, 0))), rw, tm=256, tn=LANES, tk=d, bias=rb)[:n, :N_EXPERTS]
    row_tok, row_w, blk_exp, n_used, pos = _route(logits)
    xs = jnp.take(xm, row_tok, axis=0)
    ys = _moe_experts(xs, row_w, blk_exp, n_used, layer, w1, b1, w2, b2)
    f = jnp.take(ys, pos.reshape(-1), axis=0).reshape(n, TOP_K, d)
    return f[:, 0] + f[:, 1] + f[:, 2] + f[:, 3]


def _dot_nt(a, b):
    return lax.dot_general(a, b, (((1,), (1,)), ((), ())), preferred_element_type=F32)


def _compress_body(*refs, nch):
    x_refs, (pe_ref, w_ref, o_ref) = refs[:2 * N_KV], refs[2 * N_KV:]
    for grp in range(2 * N_KV):
        sel = grp // N_KV
        acc_a = jnp.zeros((nch, HEAD_DIM), F32)
        acc_b = jnp.zeros((nch, HEAD_DIM), F32)
        for s in range(CMP_STRIDE):
            x = x_refs[grp][pl.ds(s, nch, stride=CMP_STRIDE), :]
            xa = (x + pe_ref[sel, s:s + 1, :]).astype(BF16)
            xb = (x + pe_ref[sel, CMP_STRIDE + s:CMP_STRIDE + s + 1, :]).astype(BF16)
            acc_a += jnp.dot(xa, w_ref[sel, s * HEAD_DIM:(s + 1) * HEAD_DIM, :], preferred_element_type=F32)
            acc_b += jnp.dot(xb, w_ref[sel, (CMP_STRIDE + s) * HEAD_DIM:(CMP_STRIDE + s + 1) * HEAD_DIM, :],
                             preferred_element_type=F32)
        o_ref[grp] = acc_a + pltpu.roll(acc_b, nch - 1, axis=0)


def _compress_prompt(p, bsz, t, pe, w_bf):
    nch = t // CMP_STRIDE
    x_specs = [pl.BlockSpec((t, HEAD_DIM), functools.partial(lambda b, c: (b, c), c=KV_COL // HEAD_DIM + grp))
               for grp in range(2 * N_KV)]
    return pl.pallas_call(
        functools.partial(_compress_body, nch=nch),
        grid=(bsz,),
        in_specs=x_specs + [pl.BlockSpec((2, CMP_BLOCK, HEAD_DIM), lambda b: (0, 0, 0)),
                            pl.BlockSpec((2, CMP_BLOCK * HEAD_DIM, HEAD_DIM), lambda b: (0, 0, 0))],
        out_specs=pl.BlockSpec((None, 2 * N_KV, nch, HEAD_DIM), lambda b: (b, 0, 0, 0)),
        out_shape=jax.ShapeDtypeStruct((bsz, 2 * N_KV, nch, HEAD_DIM), F32),
        compiler_params=_cparams(("parallel",)),
        name="nsa_compress",
    )(*([p] * (2 * N_KV)), pe, w_bf)


def _softmax_rows(s3, mask):
    s3 = jnp.where(mask[None], s3, NEG_INF)
    m = jnp.max(s3, axis=-1, keepdims=True)
    p = jnp.exp(s3 - m) * mask[None].astype(F32)
    return p / jnp.maximum(jnp.sum(p, axis=-1, keepdims=True), 1e-30)


def _topk_mask(score, ns, n_sel):
    lane = lax.broadcasted_iota(jnp.int32, score.shape, 1)
    cnt = jnp.zeros(score.shape, F32)
    for i in range(ns):
        c = score[:, i:i + 1]
        beats = (c > score) | ((c == score) & (lane > i))
        cnt = cnt + jnp.where(beats, 1.0, 0.0)
    return jnp.where((cnt < n_sel) & (lane < ns), 1.0, 0.0)


def _block_scores(psum, qpos, ns, nblk_lanes):
    nc = psum.shape[1]
    ci = lax.broadcasted_iota(jnp.int32, (nc, nblk_lanes), 0) * CMP_STRIDE
    bj = lax.broadcasted_iota(jnp.int32, (nc, nblk_lanes), 1) * SLC_BLOCK
    cover = jnp.where((ci < bj + SLC_BLOCK) & (ci + CMP_BLOCK > bj), 1.0, 0.0).astype(BF16)
    hi = psum.astype(BF16)
    lo = (psum - hi.astype(F32)).astype(BF16)
    imp = jnp.dot(hi, cover, preferred_element_type=F32) + jnp.dot(lo, cover, preferred_element_type=F32)
    blk = lax.broadcasted_iota(jnp.int32, (1, nblk_lanes), 1)
    cur = qpos // SLC_BLOCK
    forced = (blk == 0) | (blk == cur) | (blk == cur - 1)
    score = jnp.where(blk <= cur, imp + jnp.where(forced, FORCE_BONUS, 0.0), NEG_INF)
    return jnp.where(blk < ns, score, -3.0e38)


def _nsa_prompt_body(q_ref, g_ref, kc_ref, slc_ref, win_ref, o_ref, slc_s, win_s, m_s, l_s, acc_s,
                     *, t, tq, tk, wk):
    i = pl.program_id(1)
    nch = t // CMP_STRIDE
    nc = nch - CMP_BLOCK // CMP_STRIDE + 1
    ns = t // SLC_BLOCK
    n_sel = min(N_SLC, ns)
    scale = HEAD_DIM ** -0.5
    rows = GQA * tq

    @pl.when(i == 0)
    def _():
        slc_s[...] = slc_ref[...].astype(BF16)
        win_s[...] = win_ref[...].astype(BF16)

    q0 = i * tq
    qpos = q0 + lax.broadcasted_iota(jnp.int32, (tq, 1), 0)
    gates = jax.nn.sigmoid(g_ref[...])
    for kv in range(N_KV):
        qs = jnp.concatenate([q_ref[:, (kv * GQA + g) * HEAD_DIM:(kv * GQA + g + 1) * HEAD_DIM]
                              for g in range(GQA)], axis=0).astype(BF16)
        kc = kc_ref[kv].astype(BF16)
        vc = kc_ref[N_KV + kv].astype(BF16)
        cidx = lax.broadcasted_iota(jnp.int32, (1, nch), 1)
        cmask = (cidx * CMP_STRIDE + CMP_BLOCK - 1 <= qpos) & (cidx < nc)
        p_c = _softmax_rows((_dot_nt(qs, kc) * scale).reshape(GQA, tq, nch), cmask)
        o_c = jnp.dot(p_c.reshape(rows, nch).astype(BF16), vc, preferred_element_type=F32)
        sel = _topk_mask(_block_scores(jnp.sum(p_c, axis=0), qpos, ns, LANES), ns, n_sel).astype(BF16)

        m_s[...] = jnp.full(m_s.shape, NEG_INF, F32)
        l_s[...] = jnp.zeros(l_s.shape, F32)
        acc_s[...] = jnp.zeros(acc_s.shape, F32)

        def chunk(c, carry):
            k0 = pl.multiple_of(c * tk, tk)
            k_c = slc_s[pl.ds(k0, tk), kv * HEAD_DIM:(kv + 1) * HEAD_DIM]
            v_c = slc_s[pl.ds(k0, tk), (N_KV + kv) * HEAD_DIM:(N_KV + kv + 1) * HEAD_DIM]
            kpos = k0 + lax.broadcasted_iota(jnp.int32, (1, tk), 1)
            expand = jnp.where(lax.broadcasted_iota(jnp.int32, (LANES, tk), 0) == kpos // SLC_BLOCK,
                               1.0, 0.0).astype(BF16)
            allowed = (jnp.dot(sel, expand, preferred_element_type=F32) > 0.5) & (kpos <= qpos)
            s3 = jnp.where(allowed[None], (_dot_nt(qs, k_c) * scale).reshape(GQA, tq, tk), NEG_INF)
            m_old = m_s[...].reshape(GQA, tq, 1)
            m_new = jnp.maximum(m_old, jnp.max(s3, axis=-1, keepdims=True))
            alpha = jnp.exp(m_old - m_new)
            p = jnp.exp(s3 - m_new) * allowed[None].astype(F32)
            l_s[...] = (alpha * l_s[...].reshape(GQA, tq, 1) + jnp.sum(p, axis=-1, keepdims=True)).reshape(rows, 1)
            acc_s[...] = (alpha.reshape(rows, 1) * acc_s[...]
                          + jnp.dot(p.reshape(rows, tk).astype(BF16), v_c, preferred_element_type=F32))
            m_s[...] = m_new.reshape(rows, 1)
            return carry

        lax.fori_loop(0, ((i + 1) * tq + tk - 1) // tk, chunk, 0)
        o_s = acc_s[...] / jnp.maximum(l_s[...], 1e-30)

        k0 = pl.multiple_of(jnp.clip(q0 - WINDOW, 0, t - wk), tq)
        k_w = win_s[pl.ds(k0, wk), kv * HEAD_DIM:(kv + 1) * HEAD_DIM]
        v_w = win_s[pl.ds(k0, wk), (N_KV + kv) * HEAD_DIM:(N_KV + kv + 1) * HEAD_DIM]
        kpos = k0 + lax.broadcasted_iota(jnp.int32, (1, wk), 1)
        wmask = (kpos <= qpos) & (kpos > qpos - WINDOW)
        p_w = _softmax_rows((_dot_nt(qs, k_w) * scale).reshape(GQA, tq, wk), wmask)
        o_w = jnp.dot(p_w.reshape(rows, wk).astype(BF16), v_w, preferred_element_type=F32)

        for g in range(GQA):
            h = kv * GQA + g
            r = slice(g * tq, (g + 1) * tq)
            o = (gates[:, h:h + 1] * o_c[r] + gates[:, N_HEADS_A + h:N_HEADS_A + h + 1] * o_s[r]
                 + gates[:, 2 * N_HEADS_A + h:2 * N_HEADS_A + h + 1] * o_w[r])
            o_ref[:, h * HEAD_DIM:(h + 1) * HEAD_DIM] = o.astype(o_ref.dtype)


def _nsa_prompt(p, kc, bsz, t, *, tq=128, tk=256):
    tq, tk = min(tq, t), min(tk, t)
    wk = min(t, WINDOW + tq)
    nq = t // tq
    nch = t // CMP_STRIDE
    cw = 2 * N_KV * HEAD_DIM
    rows = GQA * tq
    return pl.pallas_call(
        functools.partial(_nsa_prompt_body, t=t, tq=tq, tk=tk, wk=wk),
        grid=(bsz, nq),
        in_specs=[pl.BlockSpec((tq, D_A), lambda b, i: (b * nq + i, 0)),
                  pl.BlockSpec((tq, LANES), lambda b, i: (b * nq + i, GATE_COL // LANES)),
                  pl.BlockSpec((None, 2 * N_KV, nch, HEAD_DIM), lambda b, i: (b, 0, 0, 0)),
                  pl.BlockSpec((t, cw), lambda b, i: (b, KV_COL // cw + 1)),
                  pl.BlockSpec((t, cw), lambda b, i: (b, KV_COL // cw + 2))],
        out_specs=pl.BlockSpec((tq, D_A), lambda b, i: (b * nq + i, 0)),
        out_shape=jax.ShapeDtypeStruct((bsz * t, D_A), BF16),
        scratch_shapes=[pltpu.VMEM((t, cw), BF16), pltpu.VMEM((t, cw), BF16),
                        pltpu.VMEM((rows, 1), F32), pltpu.VMEM((rows, 1), F32), pltpu.VMEM((rows, HEAD_DIM), F32)],
        compiler_params=_cparams(("parallel", "arbitrary")),
        name="nsa_prompt",
    )(p, p, kc, p, p)


CONV_HALO = 32


def _conformer_body(a_ref, g_ref, ah_ref, gh_ref, w_ref, cb_ref, lg_ref, lb_ref, o_ref, tail_ref, ext_s, *, tt):
    i = pl.program_id(1)
    glu = a_ref[...] * jax.nn.sigmoid(g_ref[...])
    halo = ah_ref[...] * jax.nn.sigmoid(gh_ref[...])
    ext_s[0:CONV_HALO, :] = jnp.where(i > 0, halo, 0.0)
    ext_s[CONV_HALO:, :] = glu
    y = jnp.zeros((tt, D_B), F32)
    for j in range(CONV_W):
        y = y + ext_s[pl.ds(CONV_HALO - (CONV_W - 1) + j, tt), :] * w_ref[j:j + 1, :]
    y = y + cb_ref[...]
    mu = jnp.mean(y, axis=-1, keepdims=True)
    yc = y - mu
    var = jnp.mean(yc * yc, axis=-1, keepdims=True)
    z = yc * lax.rsqrt(var + LN_EPS) * lg_ref[...] + lb_ref[...]
    o_ref[...] = (z * jax.nn.sigmoid(z)).astype(o_ref.dtype)
    tail_ref[...] = glu[tt - CONV_HALO:, :]


def _conformer_prompt(p, bsz, t, conv_w, conv_b, ln_g, ln_b, *, tt=256):
    tt = min(tt, t)
    nt = t // tt
    a_col = PB_COL // D_B
    per = tt // CONV_HALO
    wpad = jnp.pad(conv_w, ((0, CONV_HALO - CONV_W), (0, 0)))
    vec = pl.BlockSpec((1, D_B), lambda b, i: (0, 0))

    def halo_row(b, i):
        return jnp.maximum((b * nt + i) * per - 1, 0)

    return pl.pallas_call(
        functools.partial(_conformer_body, tt=tt),
        grid=(bsz, nt),
        in_specs=[pl.BlockSpec((tt, D_B), lambda b, i: (b * nt + i, a_col)),
                  pl.BlockSpec((tt, D_B), lambda b, i: (b * nt + i, a_col + 1)),
                  pl.BlockSpec((CONV_HALO, D_B), lambda b, i: (halo_row(b, i), a_col)),
                  pl.BlockSpec((CONV_HALO, D_B), lambda b, i: (halo_row(b, i), a_col + 1)),
                  pl.BlockSpec((CONV_HALO, D_B), lambda b, i: (0, 0)), vec, vec, vec],
        out_specs=[pl.BlockSpec((tt, D_B), lambda b, i: (b * nt + i, 0)),
                   pl.BlockSpec((None, CONV_HALO, D_B), lambda b, i: (b, 0, 0))],
        out_shape=[jax.ShapeDtypeStruct((bsz * t, D_B), BF16),
                   jax.ShapeDtypeStruct((bsz, CONV_HALO, D_B), F32)],
        scratch_shapes=[pltpu.VMEM((tt + CONV_HALO, D_B), F32)],
        compiler_params=_cparams(("parallel", "arbitrary")),
        name="conformer",
    )(p, p, p, p, wpad, conv_b.reshape(1, D_B), ln_g.reshape(1, D_B), ln_b.reshape(1, D_B))


def _gmlp_body(u_ref, v_ref, lg_ref, lb_ref, w_ref, bs_ref, o_ref):
    u = jax.nn.gelu(u_ref[...])
    v = jax.nn.gelu(v_ref[...])
    mu = jnp.mean(v, axis=-1, keepdims=True)
    vc = v - mu
    var = jnp.mean(vc * vc, axis=-1, keepdims=True)
    vn = (vc * lax.rsqrt(var + LN_EPS) * lg_ref[...] + lb_ref[...]).astype(BF16)
    tril = lax.broadcasted_iota(jnp.int32, (CHUNK, CHUNK), 0) >= lax.broadcasted_iota(jnp.int32, (CHUNK, CHUNK), 1)
    for h in range(N_HEADS_C):
        c = slice(h * HEAD_DIM, (h + 1) * HEAD_DIM)
        wm = jnp.where(tril, w_ref[h], 0.0).astype(BF16)
        mix = jnp.dot(wm, vn[:, c], preferred_element_type=F32) + bs_ref[:, h:h + 1]
        o_ref[:, c] = (u[:, c] * mix).astype(o_ref.dtype)


def _gmlp_prompt(p, n, ln_g, ln_b, sgu_w, sgu_b):
    u_col = PC_COL // D_C
    vec = pl.BlockSpec((1, D_C), lambda r: (0, 0))
    return pl.pallas_call(
        _gmlp_body,
        grid=(n // CHUNK,),
        in_specs=[pl.BlockSpec((CHUNK, D_C), lambda r: (r, u_col)),
                  pl.BlockSpec((CHUNK, D_C), lambda r: (r, u_col + 1)),
                  vec, vec,
                  pl.BlockSpec((N_HEADS_C, CHUNK, CHUNK), lambda r: (0, 0, 0)),
                  pl.BlockSpec((CHUNK, N_HEADS_C), lambda r: (0, 0))],
        out_specs=pl.BlockSpec((CHUNK, D_C), lambda r: (r, 0)),
        out_shape=jax.ShapeDtypeStruct((n, D_C), BF16),
        compiler_params=_cparams(("parallel",)),
        name="gmlp",
    )(p, p, ln_g.reshape(1, D_C), ln_b.reshape(1, D_C), sgu_w, sgu_b.T)


def _mixer_prompt(p, bsz, t, cmp_pe, cmp_w, conv_w, conv_b, conv_ln_g, conv_ln_b, sgu_ln_g, sgu_ln_b, sgu_w, sgu_b):
    kc = _compress_prompt(p, bsz, t, cmp_pe, cmp_w.astype(BF16))
    o_a = _nsa_prompt(p, kc, bsz, t)
    o_b, tail = _conformer_prompt(p, bsz, t, conv_w, conv_b, conv_ln_g, conv_ln_b)
    o_c = _gmlp_prompt(p, bsz * t, sgu_ln_g, sgu_ln_b, sgu_w, sgu_b)
    cat = jnp.concatenate([o_a, o_b, o_c], axis=-1)
    p3 = p.reshape(bsz, t, D_IN_PAD)
    kv_row = (2, N_KV, HEAD_DIM)
    w_keep = min(WINDOW, t)
    new_state = (p3[:, :, KV_COL:KV_COL + 2 * KV_W].reshape((bsz, t) + kv_row),
                 p3[:, :, KV_COL + 2 * KV_W:KV_COL + 4 * KV_W].reshape((bsz, t) + kv_row),
                 p3[:, t - w_keep:, KV_COL + 4 * KV_W:KV_COL + 6 * KV_W].reshape((bsz, w_keep) + kv_row),
                 tail[:, CONV_HALO - (CONV_W - 1):])
    return cat, new_state


def _layer_norm(x, g, b):
    mu = jnp.mean(x, axis=-1, keepdims=True)
    xc = x - mu
    var = jnp.mean(xc * xc, axis=-1, keepdims=True)
    return xc * lax.rsqrt(var + LN_EPS) * g + b


def _masked_softmax(s, mask):
    s = jnp.where(mask, s, NEG_INF)
    m = jnp.max(s, axis=-1, keepdims=True)
    p = jnp.exp(s - m) * mask
    return p / jnp.maximum(jnp.sum(p, axis=-1, keepdims=True), 1e-30)


def _pad_rows(x, mult):
    extra = (-x.shape[1]) % mult
    return jnp.pad(x, [(0, 0), (0, extra)] + [(0, 0)] * (x.ndim - 2))


def _split_proj(p):
    b_, t, _ = p.shape
    q = p[..., :D_A].reshape(b_, t, N_HEADS_A, HEAD_DIM)
    kv = p[..., D_A:GATE_COL].reshape(b_, t, 6, N_KV, HEAD_DIM)
    gates = jax.nn.sigmoid(p[..., GATE_COL:GATE_COL + N_BRANCH * N_HEADS_A]).reshape(b_, t, N_BRANCH, N_HEADS_A)
    pb = p[..., PB_COL:PC_COL]
    pc = p[..., PC_COL:PC_COL + 2 * D_C]
    return q, kv, gates, pb, pc


def _compress(k, pe, w):
    b_, length, kvh, hd = k.shape
    nch = length // CMP_STRIDE
    r = CMP_BLOCK // CMP_STRIDE
    nc = nch - r + 1
    ch = k.reshape(b_, nch, CMP_STRIDE, kvh, hd)
    blocks = jnp.concatenate([ch[:, s:s + nc] for s in range(r)], axis=2)
    blocks = blocks + pe[:, None, :]
    flat = jnp.swapaxes(blocks, 2, 3).reshape(b_, nc, kvh, CMP_BLOCK * hd)
    return jnp.dot(flat, w)


def _nsa_cmp_slc(q, q_pos, kc, vc, ks, vs):
    b_, t, h, hd = q.shape
    nc = kc.shape[1]
    ns = ks.shape[1] // SLC_BLOCK
    n_sel = min(N_SLC, ns)
    scale = hd ** -0.5
    ksb = ks.reshape(b_, ns, SLC_BLOCK, N_KV, hd).transpose(0, 3, 1, 2, 4)
    vsb = vs.reshape(b_, ns, SLC_BLOCK, N_KV, hd).transpose(0, 3, 1, 2, 4)
    c_start = jnp.arange(nc, dtype=jnp.int32) * CMP_STRIDE
    c_end = c_start + CMP_BLOCK - 1
    s_start = jnp.arange(ns, dtype=jnp.int32) * SLC_BLOCK
    cover = ((c_start[:, None] < s_start[None, :] + SLC_BLOCK)
             & (c_start[:, None] + CMP_BLOCK > s_start[None, :])).astype(F32)
    bi = jnp.arange(b_)[:, None, None, None]
    gi = jnp.arange(N_KV)[None, :, None, None]
    blk_ids = jnp.arange(ns, dtype=jnp.int32)

    def one_block(args):
        qb, pos = args
        nq = pos.shape[0]
        qg = qb.reshape(b_, nq, N_KV, GQA, hd)
        s_c = jnp.einsum("bqkgd,bnkd->bkgqn", qg, kc) * scale
        p_c = _masked_softmax(s_c, c_end[None, :] <= pos[:, None])
        o_c = jnp.einsum("bkgqn,bnkd->bqkgd", p_c, vc)
        imp = jnp.einsum("bkgqn,ns->bkqs", p_c, cover)
        cur = pos // SLC_BLOCK
        valid = blk_ids[None, :] <= cur[:, None]
        forced = (blk_ids[None, :] == 0) | (blk_ids[None, :] == cur[:, None]) | (blk_ids[None, :] == cur[:, None] - 1)
        score = jnp.where(valid, imp + jnp.where(forced, FORCE_BONUS, 0.0), NEG_INF)
        _, idx = lax.top_k(score, n_sel)
        kg = ksb[bi, gi, idx].reshape(b_, N_KV, nq, n_sel * SLC_BLOCK, hd)
        vg = vsb[bi, gi, idx].reshape(b_, N_KV, nq, n_sel * SLC_BLOCK, hd)
        kpos = idx[..., None] * SLC_BLOCK + jnp.arange(SLC_BLOCK, dtype=jnp.int32)
        m_s = (kpos <= pos[None, None, :, None, None]).reshape(b_, N_KV, 1, nq, n_sel * SLC_BLOCK)
        s_s = jnp.einsum("bqkgd,bkqmd->bkgqm", qg, kg) * scale
        p_s = _masked_softmax(s_s, m_s)
        o_s = jnp.einsum("bkgqm,bkqmd->bqkgd", p_s, vg)
        return o_c.reshape(b_, nq, h, hd), o_s.reshape(b_, nq, h, hd)

    qblk = SEL_QBLOCK if t % SEL_QBLOCK == 0 else t
    nblk = t // qblk
    qs = jnp.swapaxes(q.reshape(b_, nblk, qblk, h, hd), 0, 1)
    o_c, o_s = lax.map(one_block, (qs, q_pos.reshape(nblk, qblk)))
    back = lambda o: jnp.swapaxes(o, 0, 1).reshape(b_, t, h, hd)
    return back(o_c), back(o_s)


def _window_mask(tq, tk):
    return (tk <= tq) & (tk > tq - WINDOW) & (tk >= 0)


def _window_attn_direct(q, q_pos, k, v, k_pos):
    b_, t, h, hd = q.shape
    qg = q.reshape(b_, t, N_KV, GQA, hd)
    s = jnp.einsum("bqkgd,bmkd->bkgqm", qg, k) * hd ** -0.5
    p = _masked_softmax(s, _window_mask(q_pos[:, None], k_pos[None, :]))
    o = jnp.einsum("bkgqm,bmkd->bqkgd", p, v)
    return o.reshape(b_, t, h, hd)


def _nsa_merge(gates, o_cmp, o_slc, o_win):
    o = (gates[:, :, 0, :, None] * o_cmp + gates[:, :, 1, :, None] * o_slc
         + gates[:, :, 2, :, None] * o_win)
    return o.reshape(o.shape[0], o.shape[1], D_A)


def _glu_in(pb):
    a, g = jnp.split(pb, 2, axis=-1)
    return a * jax.nn.sigmoid(g)


def _conformer_conv(glu_ext, conv_w, conv_b, ln_g, ln_b):
    y = lax.conv_general_dilated(glu_ext, conv_w[:, None, :], window_strides=(1,),
                                 padding="VALID", dimension_numbers=("NWC", "WIO", "NWC"),
                                 feature_group_count=D_B)
    return jax.nn.silu(_layer_norm(y + conv_b, ln_g, ln_b))


def _sgu_split(pc, ln_g, ln_b):
    u, v = jnp.split(jax.nn.gelu(pc), 2, axis=-1)
    return u, _layer_norm(v, ln_g, ln_b)


def _chunk_mix(v, w_s, b_s):
    b_, length, _ = v.shape
    vh = v.reshape(b_, length // CHUNK, CHUNK, N_HEADS_C, HEAD_DIM)
    wm = w_s * jnp.tril(jnp.ones((CHUNK, CHUNK), w_s.dtype))
    out = jnp.einsum("hts,bnshd->bnthd", wm, vh) + jnp.swapaxes(b_s, 0, 1)[None, None, :, :, None]
    return out.reshape(b_, length, D_C)


def _mixer_sample(p, past_cmp, past_slc, win_buf, conv_buf, cmp_pe, cmp_w, conv_w, conv_b,
                  conv_ln_g, conv_ln_b, sgu_ln_g, sgu_ln_b, sgu_w, sgu_b):
    b_, t, _ = p.shape
    n_past = past_cmp.shape[1]
    w_buf = win_buf.shape[1]
    q, kv, gates, pb, pc = _split_proj(p)
    pos = n_past + jnp.arange(t, dtype=jnp.int32)
    full_c = _pad_rows(jnp.concatenate([past_cmp, kv[:, :, 0:2]], axis=1), CMP_STRIDE)
    kc = _compress(full_c[:, :, 0], cmp_pe[0], cmp_w[0])
    vc = _compress(full_c[:, :, 1], cmp_pe[1], cmp_w[1])
    full_s = _pad_rows(jnp.concatenate([past_slc, kv[:, :, 2:4]], axis=1), SLC_BLOCK)
    o_cmp, o_slc = _nsa_cmp_slc(q, pos, kc, vc, full_s[:, :, 0], full_s[:, :, 1])
    win = jnp.concatenate([win_buf, kv[:, :, 4:6]], axis=1)
    kpos = n_past - w_buf + jnp.arange(w_buf + t, dtype=jnp.int32)
    o_win = _window_attn_direct(q, pos, win[:, :, 0], win[:, :, 1], kpos)
    o_a = _nsa_merge(gates, o_cmp, o_slc, o_win)
    glu = _glu_in(pb)
    glu_ext = jnp.concatenate([conv_buf, glu], axis=1)
    o_b = _conformer_conv(glu_ext, conv_w, conv_b, conv_ln_g, conv_ln_b)
    u, v = _sgu_split(pc, sgu_ln_g, sgu_ln_b)
    o_c = u * _chunk_mix(_pad_rows(v, CHUNK), sgu_w, sgu_b)[:, :t]
    cat = jnp.concatenate([o_a, o_b, o_c], axis=-1).astype(BF16)
    new_state = (kv[:, :, 0:2], kv[:, :, 2:4], win[:, t:], glu_ext[:, t:], v)
    return cat, new_state


def _pad_w_in(w):
    k = w.shape[0]
    g = N_BRANCH * N_HEADS_A
    return jnp.concatenate([
        w[:, :GATE_COL + g], jnp.zeros((k, PB_COL - GATE_COL - g), w.dtype), w[:, GATE_COL + g:]],
        axis=1).astype(BF16)


def kernel(x_prompt, x_sample, cache_kv_cmp, cache_kv_slc, cache_kv_win, state_conv, page_table,
           c_prompt, c_sample, w_in, w_out, cmp_pe, cmp_w, conv_w, conv_b, conv_ln_g, conv_ln_b,
           sgu_ln_g, sgu_ln_b, sgu_w, sgu_b, ada_w, ada_b, ln_g, ln_b, router_w, router_b,
           moe_w1, moe_b1, moe_w2, moe_b2):
    bp, tp, d = x_prompt.shape
    bs, ts, _ = x_sample.shape
    n_p, n_s = bp * tp, bs * ts
    xp = x_prompt.reshape(n_p, d)
    xs = x_sample.reshape(n_s, d)
    c_all = jnp.concatenate([c_prompt, c_sample], axis=0)
    c_all = jax.nn.silu(jnp.pad(c_all, ((0, 16 - bp - bs), (0, 0))))
    st = [[] for _ in range(9)]
    for l in range(DEPTH):
        mod = _matmul(c_all, ada_w[l], tm=16, tn=2048, tk=1024, bias=ada_b[l])
        mod_p = mod[:bp].reshape(bp, 1, 6, d)
        mod_s = jnp.repeat(mod[bp:bp + bs], ts, axis=0).reshape(1, n_s, 6, d)
        mp = [mod_p[:, :, i] for i in range(6)]
        ms = [mod_s[:, :, i] for i in range(6)]
        w_in_l = _pad_w_in(w_in[l])
        w_out_l = w_out[l].astype(BF16)
        mix_w = (cmp_pe[l], cmp_w[l], conv_w[l], conv_b[l], conv_ln_g[l], conv_ln_b[l],
                 sgu_ln_g[l], sgu_ln_b[l], sgu_w[l], sgu_b[l])

        pp = _matmul(xp, w_in_l, tm=1024, tn=1024, tk=512, mod=(mp[1], mp[0]))
        ps = _matmul(xs, w_in_l, tm=n_s, tn=1024, tk=512, mod=(ms[1], ms[0]))
        cat_p, st_p = _mixer_prompt(pp, bp, tp, *mix_w)
        past_c = cache_kv_cmp[l, page_table].reshape(bs, -1, 2, N_KV, HEAD_DIM)
        past_s = cache_kv_slc[l, page_table].reshape(bs, -1, 2, N_KV, HEAD_DIM)
        cat_s, st_s = _mixer_sample(ps.reshape(bs, ts, D_IN_PAD), past_c, past_s, cache_kv_win[l], state_conv[l], *mix_w)
        m_p = _matmul(cat_p, w_out_l, tm=1024, tn=1024, tk=512)
        m_s = _matmul(cat_s.reshape(n_s, D_MIX), w_out_l, tm=n_s, tn=1024, tk=512)
        xp, xmp = _deepnorm(xp, m_p, mp[2], ln_g[l, 0], ln_b[l, 0], mod=(mp[4], mp[3]))
        xs, xms = _deepnorm(xs, m_s, ms[2], ln_g[l, 0], ln_b[l, 0], mod=(ms[4], ms[3]), tm=n_s)
        f = _moe(jnp.concatenate([xmp, xms], axis=0), l, router_w[l], router_b[l], moe_w1, moe_b1, moe_w2, moe_b2)
        xp = _deepnorm(xp, f[:n_p], mp[5], ln_g[l, 1], ln_b[l, 1])
        xs = _deepnorm(xs, f[n_p:], ms[5], ln_g[l, 1], ln_b[l, 1], tm=n_s)
        for i in range(4):
            st[2 * i].append(st_p[i])
            st[2 * i + 1].append(st_s[i])
        st[8].append(st_s[4])
    return (xp.reshape(bp, tp, d), xs.reshape(bs, ts, d)) + tuple(jnp.stack(s) for s in st)
```

```python
import functools

import jax
import jax.numpy as jnp
import numpy as np
from jax import lax
from jax.experimental import pallas as pl
from jax.experimental.pallas import tpu as pltpu

F32 = jnp.float32
BF16 = jnp.bfloat16

D_MODEL = 4096
DEPTH = 2
HEAD_DIM = 128
D_A = D_MODEL // 2
D_B = D_MODEL // 4
D_C = D_MODEL // 4
D_MIX = D_A + D_B + D_C
N_HEADS_A = D_A // HEAD_DIM
N_KV = 2
GQA = N_HEADS_A // N_KV
KV_W = N_KV * HEAD_DIM
N_BRANCH = 3
N_HEADS_C = D_C // HEAD_DIM
CMP_BLOCK = 32
CMP_STRIDE = 16
SLC_BLOCK = 64
N_SLC = 16
WINDOW = 512
SEL_QBLOCK = 64
WIN_QBLOCK = 128
FORCE_BONUS = 1.0e4
CONV_W = 31
CHUNK = 128
N_EXPERTS = 32
TOP_K = 4
D_FF = D_MODEL // 2
SWIGLU_ALPHA = 1.702
SWIGLU_LIMIT = 7.0
DEEPNORM_ALPHA = (2 * DEPTH) ** 0.25
LN_EPS = 1e-5
NEG_INF = -1e30
SPLITS = (D_A, 6 * KV_W, N_BRANCH * N_HEADS_A, 2 * D_B, 2 * D_C)
D_IN = sum(SPLITS)

LANES = 128
V7X_VMEM_BYTES = 64 * 1024 * 1024
VMEM_LIMIT = 56 * 1024 * 1024

KV_COL = D_A
GATE_COL = D_A + 6 * KV_W
PB_COL = D_MODEL
PC_COL = PB_COL + 2 * D_B
D_IN_PAD = PC_COL + 2 * D_C

MOE_TM = 256


def _cparams(sem):
    return pltpu.CompilerParams(dimension_semantics=sem, vmem_limit_bytes=VMEM_LIMIT)


def _mm_body(*refs, n_k, modulate, has_bias):
    it = iter(refs)
    x_ref = next(it)
    sc_ref = next(it) if modulate else None
    sh_ref = next(it) if modulate else None
    w_ref = next(it)
    b_ref = next(it) if has_bias else None
    o_ref = next(it)
    acc_ref = next(it)
    k = pl.program_id(2)

    @pl.when(k == 0)
    def _():
        acc_ref[...] = jnp.zeros_like(acc_ref)

    x = x_ref[...]
    if modulate:
        x = x.astype(F32) * (1.0 + sc_ref[...]) + sh_ref[...]
    acc_ref[...] += jnp.dot(x.astype(BF16), w_ref[...].astype(BF16), preferred_element_type=F32)

    @pl.when(k == n_k - 1)
    def _():
        r = acc_ref[...]
        if has_bias:
            r = r + b_ref[...]
        o_ref[...] = r.astype(o_ref.dtype)


def _matmul(x, w, *, tm, tn, tk, out_dtype=F32, bias=None, mod=None):
    m, kk = x.shape
    n = w.shape[1]
    tm, tn, tk = min(tm, m), min(tn, n), min(tk, kk)
    assert m % tm == 0 and n % tn == 0 and kk % tk == 0
    n_k = kk // tk
    in_specs = [pl.BlockSpec((tm, tk), lambda i, j, k: (i, k))]
    args = [x]
    if mod is not None:
        g, r, _ = mod[0].shape
        rows = m // g
        assert rows % tm == 0 and r in (1, tm) and (r == 1 or rows == tm)
        per = rows // tm
        spec = pl.BlockSpec((None, r, tk), lambda i, j, k: (i // per, 0, k))
        in_specs += [spec, spec]
        args += [mod[0], mod[1]]
    in_specs.append(pl.BlockSpec((tk, tn), lambda i, j, k: (k, j)))
    args.append(w)
    if bias is not None:
        in_specs.append(pl.BlockSpec((1, tn), lambda i, j, k: (0, j)))
        args.append(bias.reshape(1, n))
    return pl.pallas_call(
        functools.partial(_mm_body, n_k=n_k, modulate=mod is not None, has_bias=bias is not None),
        grid=(m // tm, n // tn, n_k),
        in_specs=in_specs,
        out_specs=pl.BlockSpec((tm, tn), lambda i, j, k: (i, j)),
        out_shape=jax.ShapeDtypeStruct((m, n), out_dtype),
        scratch_shapes=[pltpu.VMEM((tm, tn), F32)],
        compiler_params=_cparams(("parallel", "parallel", "arbitrary")),
        name="matmul",
    )(*args)


def _dn_body(*refs, modulate):
    it = iter(refs)
    x_ref, m_ref, gate_ref, g_ref, b_ref = (next(it) for _ in range(5))
    sc_ref = next(it) if modulate else None
    sh_ref = next(it) if modulate else None
    y_ref = next(it)
    ym_ref = next(it) if modulate else None
    z = DEEPNORM_ALPHA * x_ref[...] + (1.0 + gate_ref[...]) * m_ref[...]
    mu = jnp.mean(z, axis=-1, keepdims=True)
    zc = z - mu
    var = jnp.mean(zc * zc, axis=-1, keepdims=True)
    y = zc * lax.rsqrt(var + LN_EPS) * g_ref[...] + b_ref[...]
    y_ref[...] = y
    if modulate:
        ym_ref[...] = (y * (1.0 + sc_ref[...]) + sh_ref[...]).astype(ym_ref.dtype)


def _deepnorm(x, m, gate, ln_g, ln_b, mod=None, *, tm=256):
    n, d = x.shape
    tm = min(tm, n)
    assert n % tm == 0
    g, r, _ = gate.shape
    rows = n // g
    assert rows % tm == 0 and r in (1, tm) and (r == 1 or rows == tm)
    per = rows // tm
    row = pl.BlockSpec((tm, d), lambda i: (i, 0))
    grp = pl.BlockSpec((None, r, d), lambda i: (i // per, 0, 0))
    vec = pl.BlockSpec((1, d), lambda i: (0, 0))
    in_specs = [row, row, grp, vec, vec]
    args = [x, m, gate, ln_g.reshape(1, d), ln_b.reshape(1, d)]
    out_specs = [row]
    out_shape = [jax.ShapeDtypeStruct((n, d), F32)]
    if mod is not None:
        in_specs += [grp, grp]
        args += [mod[0], mod[1]]
        out_specs.append(row)
        out_shape.append(jax.ShapeDtypeStruct((n, d), BF16))
    out = pl.pallas_call(
        functools.partial(_dn_body, modulate=mod is not None),
        grid=(n // tm,),
        in_specs=in_specs,
        out_specs=out_specs,
        out_shape=out_shape,
        compiler_params=_cparams(("parallel",)),
        name="deepnorm",
    )(*args)
    return out if mod is not None else out[0]


def _new_expert(be_ref, i):
    return (i == 0) | (be_ref[i] != be_ref[jnp.maximum(i - 1, 0)])


def _gmm1_body(be_ref, nu_ref, x_ref, wg_ref, wl_ref, bg_ref, bl_ref, o_ref, wg_s, wl_s):
    i = pl.program_id(1)

    @pl.when(_new_expert(be_ref, i))
    def _():
        wg_s[...] = wg_ref[...].astype(BF16)
        wl_s[...] = wl_ref[...].astype(BF16)

    @pl.when(i < nu_ref[0])
    def _():
        x = x_ref[...]
        hg = jnp.dot(x, wg_s[...], preferred_element_type=F32) + bg_ref[...]
        hl = jnp.dot(x, wl_s[...], preferred_element_type=F32) + bl_ref[...]
        hg = jnp.minimum(hg, SWIGLU_LIMIT)
        hl = jnp.clip(hl, -SWIGLU_LIMIT, SWIGLU_LIMIT)
        act = hg * jax.nn.sigmoid(SWIGLU_ALPHA * hg) * (hl + 1.0)
        o_ref[...] = act.astype(o_ref.dtype)


def _gmm2_body(be_ref, nu_ref, a_ref, w_ref, b_ref, rw_ref, o_ref, w_s):
    i = pl.program_id(1)

    @pl.when(_new_expert(be_ref, i))
    def _():
        w_s[...] = w_ref[...].astype(BF16)

    @pl.when(i < nu_ref[0])
    def _():
        y = jnp.dot(a_ref[...], w_s[...], preferred_element_type=F32) + b_ref[...]
        o_ref[...] = y * rw_ref[...]


def _moe_experts(xs, row_w, blk_exp, n_used, layer, w1, b1, w2, b2, *, tn1=512, tn2=2048):
    cap, d = xs.shape
    tm = MOE_TM
    n_blk = cap // tm
    nj1 = D_FF // tn1
    nj2 = d // tn2
    b1r = b1.reshape(DEPTH, N_EXPERTS, 1, 2 * D_FF)
    b2r = b2.reshape(DEPTH, N_EXPERTS, 1, d)

    def row_blk(i, nu):
        return jnp.minimum(i, nu[0] - 1)

    act = pl.pallas_call(
        _gmm1_body,
        grid_spec=pltpu.PrefetchScalarGridSpec(
            num_scalar_prefetch=2,
            grid=(nj1, n_blk),
            in_specs=[
                pl.BlockSpec((tm, d), lambda j, i, be, nu: (row_blk(i, nu), 0)),
                pl.BlockSpec((None, None, d, tn1), lambda j, i, be, nu: (layer, be[i], 0, j)),
                pl.BlockSpec((None, None, d, tn1), lambda j, i, be, nu: (layer, be[i], 0, nj1 + j)),
                pl.BlockSpec((None, None, 1, tn1), lambda j, i, be, nu: (layer, be[i], 0, j)),
                pl.BlockSpec((None, None, 1, tn1), lambda j, i, be, nu: (layer, be[i], 0, nj1 + j)),
            ],
            out_specs=pl.BlockSpec((tm, tn1), lambda j, i, be, nu: (row_blk(i, nu), j)),
            scratch_shapes=[pltpu.VMEM((d, tn1), BF16), pltpu.VMEM((d, tn1), BF16)],
        ),
        out_shape=jax.ShapeDtypeStruct((cap, D_FF), BF16),
        compiler_params=_cparams(("arbitrary", "arbitrary")),
        name="moe_gmm1",
    )(blk_exp, n_used, xs, w1, w1, b1r, b1r)

    ys = pl.pallas_call(
        _gmm2_body,
        grid_spec=pltpu.PrefetchScalarGridSpec(
            num_scalar_prefetch=2,
            grid=(nj2, n_blk),
            in_specs=[
                pl.BlockSpec((tm, D_FF), lambda j, i, be, nu: (row_blk(i, nu), 0)),
                pl.BlockSpec((None, None, D_FF, tn2), lambda j, i, be, nu: (layer, be[i], 0, j)),
                pl.BlockSpec((None, None, 1, tn2), lambda j, i, be, nu: (layer, be[i], 0, j)),
                pl.BlockSpec((tm, 1), lambda j, i, be, nu: (row_blk(i, nu), 0)),
            ],
            out_specs=pl.BlockSpec((tm, tn2), lambda j, i, be, nu: (row_blk(i, nu), j)),
            scratch_shapes=[pltpu.VMEM((D_FF, tn2), BF16)],
        ),
        out_shape=jax.ShapeDtypeStruct((cap, d), F32),
        compiler_params=_cparams(("arbitrary", "arbitrary")),
        name="moe_gmm2",
    )(blk_exp, n_used, act, w2, b2r, row_w)
    return ys


def _route(logits):
    n = logits.shape[0]
    tm = MOE_TM
    cap = -(-(n * TOP_K + N_EXPERTS * (tm - 1)) // tm) * tm
    n_blk = cap // tm
    top_val, top_exp = lax.top_k(logits, TOP_K)
    weights = jax.nn.softmax(top_val, axis=-1)
    onehot = (top_exp[:, :, None] == jnp.arange(N_EXPERTS, dtype=jnp.int32)).any(axis=1).astype(jnp.int32)
    counts = onehot.sum(axis=0)
    padded = (counts + tm - 1) // tm * tm
    pad_end = jnp.cumsum(padded)
    pad_start = pad_end - padded
    rank = jnp.cumsum(onehot, axis=0) - onehot
    pos = pad_start[top_exp] + jnp.take_along_axis(rank, top_exp, axis=1)
    tok = jnp.broadcast_to(jnp.arange(n, dtype=jnp.int32)[:, None], (n, TOP_K))
    row_tok = jnp.zeros((cap,), jnp.int32).at[pos.reshape(-1)].set(tok.reshape(-1))
    row_w = jnp.zeros((cap,), F32).at[pos.reshape(-1)].set(weights.reshape(-1))
    blk_exp = jnp.minimum(jnp.searchsorted(pad_end, jnp.arange(n_blk, dtype=jnp.int32) * tm, side="right"),
                          N_EXPERTS - 1).astype(jnp.int32)
    n_used = (pad_end[-1:] // tm).astype(jnp.int32)
    return row_tok, row_w.reshape(cap, 1), blk_exp, n_used, pos


def _moe(xm, layer, router_w, router_b, w1, b1, w2, b2):
    n, d = xm.shape
    rw = jnp.pad(router_w, ((0, 0), (0, LANES - N_EXPERTS))).astype(BF16)
    rb = jnp.pad(router_b, (0, LANES - N_EXPERTS))
    tm = next((c for c in range(min(n, 4096) // 16 * 16, 0, -16) if n % c == 0), n)
    logits = _matmul(xm, rw, tm=tm, tn=LANES, tk=1024, bias=rb)[:, :N_EXPERTS]
    row_tok, row_w, blk_exp, n_used, pos = _route(logits)
    xs = jnp.take(xm, row_tok, axis=0)
    ys = _moe_experts(xs, row_w, blk_exp, n_used, layer, w1, b1, w2, b2)
    f = jnp.take(ys, pos.reshape(-1), axis=0).reshape(n, TOP_K, d)
    return f[:, 0] + f[:, 1] + f[:, 2] + f[:, 3]


def _dot_nt(a, b):
    return lax.dot_general(a, b, (((1,), (1,)), ((), ())), preferred_element_type=F32)


def _compress_body(*refs, nch):
    x_refs, (pe_ref, w_ref, o_ref) = refs[:2 * N_KV], refs[2 * N_KV:]
    for grp in range(2 * N_KV):
        sel = grp // N_KV
        acc_a = jnp.zeros((nch, HEAD_DIM), F32)
        acc_b = jnp.zeros((nch, HEAD_DIM), F32)
        for s in range(CMP_STRIDE):
            x = x_refs[grp][pl.ds(s, nch, stride=CMP_STRIDE), :]
            xa = (x + pe_ref[sel, s:s + 1, :]).astype(BF16)
            xb = (x + pe_ref[sel, CMP_STRIDE + s:CMP_STRIDE + s + 1, :]).astype(BF16)
            acc_a += jnp.dot(xa, w_ref[sel, s * HEAD_DIM:(s + 1) * HEAD_DIM, :], preferred_element_type=F32)
            acc_b += jnp.dot(xb, w_ref[sel, (CMP_STRIDE + s) * HEAD_DIM:(CMP_STRIDE + s + 1) * HEAD_DIM, :],
                             preferred_element_type=F32)
        o_ref[grp] = acc_a + pltpu.roll(acc_b, nch - 1, axis=0)


def _compress_prompt(p, bsz, t, pe, w_bf):
    nch = t // CMP_STRIDE
    x_specs = [pl.BlockSpec((t, HEAD_DIM), functools.partial(lambda b, c: (b, c), c=KV_COL // HEAD_DIM + grp))
               for grp in range(2 * N_KV)]
    return pl.pallas_call(
        functools.partial(_compress_body, nch=nch),
        grid=(bsz,),
        in_specs=x_specs + [pl.BlockSpec((2, CMP_BLOCK, HEAD_DIM), lambda b: (0, 0, 0)),
                            pl.BlockSpec((2, CMP_BLOCK * HEAD_DIM, HEAD_DIM), lambda b: (0, 0, 0))],
        out_specs=pl.BlockSpec((None, 2 * N_KV, nch, HEAD_DIM), lambda b: (b, 0, 0, 0)),
        out_shape=jax.ShapeDtypeStruct((bsz, 2 * N_KV, nch, HEAD_DIM), F32),
        compiler_params=_cparams(("parallel",)),
        name="nsa_compress",
    )(*([p] * (2 * N_KV)), pe, w_bf)


def _softmax_rows(s3, mask):
    s3 = jnp.where(mask[None], s3, NEG_INF)
    m = jnp.max(s3, axis=-1, keepdims=True)
    p = jnp.exp(s3 - m) * mask[None].astype(F32)
    return p / jnp.maximum(jnp.sum(p, axis=-1, keepdims=True), 1e-30)


def _topk_mask(score, ns, n_sel):
    lane = lax.broadcasted_iota(jnp.int32, score.shape, 1)
    cnt = jnp.zeros(score.shape, F32)
    for i in range(ns):
        c = score[:, i:i + 1]
        beats = (c > score) | ((c == score) & (lane > i))
        cnt = cnt + jnp.where(beats, 1.0, 0.0)
    return jnp.where((cnt < n_sel) & (lane < ns), 1.0, 0.0)


def _block_scores(psum, qpos, ns, nblk_lanes):
    nc = psum.shape[1]
    ci = lax.broadcasted_iota(jnp.int32, (nc, nblk_lanes), 0) * CMP_STRIDE
    bj = lax.broadcasted_iota(jnp.int32, (nc, nblk_lanes), 1) * SLC_BLOCK
    cover = jnp.where((ci < bj + SLC_BLOCK) & (ci + CMP_BLOCK > bj), 1.0, 0.0).astype(BF16)
    hi = psum.astype(BF16)
    lo = (psum - hi.astype(F32)).astype(BF16)
    imp = jnp.dot(hi, cover, preferred_element_type=F32) + jnp.dot(lo, cover, preferred_element_type=F32)
    blk = lax.broadcasted_iota(jnp.int32, (1, nblk_lanes), 1)
    cur = qpos // SLC_BLOCK
    forced = (blk == 0) | (blk == cur) | (blk == cur - 1)
    score = jnp.where(blk <= cur, imp + jnp.where(forced, FORCE_BONUS, 0.0), NEG_INF)
    return jnp.where(blk < ns, score, -3.0e38)


def _nsa_prompt_body(q_ref, g_ref, kc_ref, slc_ref, win_ref, o_ref, slc_s, win_s, m_s, l_s, acc_s,
                     *, t, tq, tk, wk):
    i = pl.program_id(1)
    nch = t // CMP_STRIDE
    nc = nch - CMP_BLOCK // CMP_STRIDE + 1
    ns = t // SLC_BLOCK
    n_sel = min(N_SLC, ns)
    scale = HEAD_DIM ** -0.5
    rows = GQA * tq

    @pl.when(i == 0)
    def _():
        slc_s[...] = slc_ref[...].astype(BF16)
        win_s[...] = win_ref[...].astype(BF16)

    q0 = i * tq
    qpos = q0 + lax.broadcasted_iota(jnp.int32, (tq, 1), 0)
    gates = jax.nn.sigmoid(g_ref[...])
    for kv in range(N_KV):
        qs = jnp.concatenate([q_ref[:, (kv * GQA + g) * HEAD_DIM:(kv * GQA + g + 1) * HEAD_DIM]
                              for g in range(GQA)], axis=0).astype(BF16)
        kc = kc_ref[kv].astype(BF16)
        vc = kc_ref[N_KV + kv].astype(BF16)
        cidx = lax.broadcasted_iota(jnp.int32, (1, nch), 1)
        cmask = (cidx * CMP_STRIDE + CMP_BLOCK - 1 <= qpos) & (cidx < nc)
        p_c = _softmax_rows((_dot_nt(qs, kc) * scale).reshape(GQA, tq, nch), cmask)
        o_c = jnp.dot(p_c.reshape(rows, nch).astype(BF16), vc, preferred_element_type=F32)
        sel = _topk_mask(_block_scores(jnp.sum(p_c, axis=0), qpos, ns, LANES), ns, n_sel).astype(BF16)

        m_s[...] = jnp.full(m_s.shape, NEG_INF, F32)
        l_s[...] = jnp.zeros(l_s.shape, F32)
        acc_s[...] = jnp.zeros(acc_s.shape, F32)

        def chunk(c, carry):
            k0 = pl.multiple_of(c * tk, tk)
            k_c = slc_s[pl.ds(k0, tk), kv * HEAD_DIM:(kv + 1) * HEAD_DIM]
            v_c = slc_s[pl.ds(k0, tk), (N_KV + kv) * HEAD_DIM:(N_KV + kv + 1) * HEAD_DIM]
            kpos = k0 + lax.broadcasted_iota(jnp.int32, (1, tk), 1)
            expand = jnp.where(lax.broadcasted_iota(jnp.int32, (LANES, tk), 0) == kpos // SLC_BLOCK,
                               1.0, 0.0).astype(BF16)
            allowed = (jnp.dot(sel, expand, preferred_element_type=F32) > 0.5) & (kpos <= qpos)
            s3 = jnp.where(allowed[None], (_dot_nt(qs, k_c) * scale).reshape(GQA, tq, tk), NEG_INF)
            m_old = m_s[...].reshape(GQA, tq, 1)
            m_new = jnp.maximum(m_old, jnp.max(s3, axis=-1, keepdims=True))
            alpha = jnp.exp(m_old - m_new)
            p = jnp.exp(s3 - m_new) * allowed[None].astype(F32)
            l_s[...] = (alpha * l_s[...].reshape(GQA, tq, 1) + jnp.sum(p, axis=-1, keepdims=True)).reshape(rows, 1)
            acc_s[...] = (alpha.reshape(rows, 1) * acc_s[...]
                          + jnp.dot(p.reshape(rows, tk).astype(BF16), v_c, preferred_element_type=F32))
            m_s[...] = m_new.reshape(rows, 1)
            return carry

        lax.fori_loop(0, ((i + 1) * tq + tk - 1) // tk, chunk, 0)
        o_s = acc_s[...] / jnp.maximum(l_s[...], 1e-30)

        k0 = pl.multiple_of(jnp.clip(q0 - WINDOW, 0, t - wk), tq)
        k_w = win_s[pl.ds(k0, wk), kv * HEAD_DIM:(kv + 1) * HEAD_DIM]
        v_w = win_s[pl.ds(k0, wk), (N_KV + kv) * HEAD_DIM:(N_KV + kv + 1) * HEAD_DIM]
        kpos = k0 + lax.broadcasted_iota(jnp.int32, (1, wk), 1)
        wmask = (kpos <= qpos) & (kpos > qpos - WINDOW)
        p_w = _softmax_rows((_dot_nt(qs, k_w) * scale).reshape(GQA, tq, wk), wmask)
        o_w = jnp.dot(p_w.reshape(rows, wk).astype(BF16), v_w, preferred_element_type=F32)

        for g in range(GQA):
            h = kv * GQA + g
            r = slice(g * tq, (g + 1) * tq)
            o = (gates[:, h:h + 1] * o_c[r] + gates[:, N_HEADS_A + h:N_HEADS_A + h + 1] * o_s[r]
                 + gates[:, 2 * N_HEADS_A + h:2 * N_HEADS_A + h + 1] * o_w[r])
            o_ref[:, h * HEAD_DIM:(h + 1) * HEAD_DIM] = o.astype(o_ref.dtype)


def _nsa_prompt(p, kc, bsz, t, *, tq=128, tk=256):
    tq, tk = min(tq, t), min(tk, t)
    wk = min(t, WINDOW + tq)
    nq = t // tq
    nch = t // CMP_STRIDE
    cw = 2 * N_KV * HEAD_DIM
    rows = GQA * tq
    return pl.pallas_call(
        functools.partial(_nsa_prompt_body, t=t, tq=tq, tk=tk, wk=wk),
        grid=(bsz, nq),
        in_specs=[pl.BlockSpec((tq, D_A), lambda b, i: (b * nq + i, 0)),
                  pl.BlockSpec((tq, LANES), lambda b, i: (b * nq + i, GATE_COL // LANES)),
                  pl.BlockSpec((None, 2 * N_KV, nch, HEAD_DIM), lambda b, i: (b, 0, 0, 0)),
                  pl.BlockSpec((t, cw), lambda b, i: (b, KV_COL // cw + 1)),
                  pl.BlockSpec((t, cw), lambda b, i: (b, KV_COL // cw + 2))],
        out_specs=pl.BlockSpec((tq, D_A), lambda b, i: (b * nq + i, 0)),
        out_shape=jax.ShapeDtypeStruct((bsz * t, D_A), BF16),
        scratch_shapes=[pltpu.VMEM((t, cw), BF16), pltpu.VMEM((t, cw), BF16),
                        pltpu.VMEM((rows, 1), F32), pltpu.VMEM((rows, 1), F32), pltpu.VMEM((rows, HEAD_DIM), F32)],
        compiler_params=_cparams(("parallel", "arbitrary")),
        name="nsa_prompt",
    )(p, p, kc, p, p)


CONV_HALO = 32


def _conformer_body(a_ref, g_ref, ah_ref, gh_ref, w_ref, cb_ref, lg_ref, lb_ref, o_ref, tail_ref, ext_s,
                    *, tt, state_halo):
    glu = a_ref[...] * jax.nn.sigmoid(g_ref[...])
    if state_halo:
        ext_s[0:CONV_HALO, :] = ah_ref[...]
    else:
        halo = ah_ref[...] * jax.nn.sigmoid(gh_ref[...])
        ext_s[0:CONV_HALO, :] = jnp.where(pl.program_id(1) > 0, halo, 0.0)
    ext_s[CONV_HALO:, :] = glu
    y = jnp.zeros((tt, D_B), F32)
    for j in range(CONV_W):
        y = y + ext_s[pl.ds(CONV_HALO - (CONV_W - 1) + j, tt), :] * w_ref[j:j + 1, :]
    y = y + cb_ref[...]
    mu = jnp.mean(y, axis=-1, keepdims=True)
    yc = y - mu
    var = jnp.mean(yc * yc, axis=-1, keepdims=True)
    z = yc * lax.rsqrt(var + LN_EPS) * lg_ref[...] + lb_ref[...]
    o_ref[...] = (z * jax.nn.sigmoid(z)).astype(o_ref.dtype)
    tail_ref[...] = ext_s[tt:tt + CONV_HALO, :]


def _conformer_prompt(p, bsz, t, conv_w, conv_b, ln_g, ln_b, *, tt=256):
    tt = min(tt, t)
    nt = t // tt
    a_col = PB_COL // D_B
    per = tt // CONV_HALO
    wpad = jnp.pad(conv_w, ((0, CONV_HALO - CONV_W), (0, 0)))
    vec = pl.BlockSpec((1, D_B), lambda b, i: (0, 0))

    def halo_row(b, i):
        return jnp.maximum((b * nt + i) * per - 1, 0)

    return pl.pallas_call(
        functools.partial(_conformer_body, tt=tt, state_halo=False),
        grid=(bsz, nt),
        in_specs=[pl.BlockSpec((tt, D_B), lambda b, i: (b * nt + i, a_col)),
                  pl.BlockSpec((tt, D_B), lambda b, i: (b * nt + i, a_col + 1)),
                  pl.BlockSpec((CONV_HALO, D_B), lambda b, i: (halo_row(b, i), a_col)),
                  pl.BlockSpec((CONV_HALO, D_B), lambda b, i: (halo_row(b, i), a_col + 1)),
                  pl.BlockSpec((CONV_HALO, D_B), lambda b, i: (0, 0)), vec, vec, vec],
        out_specs=[pl.BlockSpec((tt, D_B), lambda b, i: (b * nt + i, 0)),
                   pl.BlockSpec((None, CONV_HALO, D_B), lambda b, i: (b, 0, 0))],
        out_shape=[jax.ShapeDtypeStruct((bsz * t, D_B), BF16),
                   jax.ShapeDtypeStruct((bsz, CONV_HALO, D_B), F32)],
        scratch_shapes=[pltpu.VMEM((tt + CONV_HALO, D_B), F32)],
        compiler_params=_cparams(("parallel", "arbitrary")),
        name="conformer",
    )(p, p, p, p, wpad, conv_b.reshape(1, D_B), ln_g.reshape(1, D_B), ln_b.reshape(1, D_B))


def _gmlp_body(u_ref, v_ref, lg_ref, lb_ref, w_ref, bs_ref, o_ref):
    u = jax.nn.gelu(u_ref[...])
    v = jax.nn.gelu(v_ref[...])
    mu = jnp.mean(v, axis=-1, keepdims=True)
    vc = v - mu
    var = jnp.mean(vc * vc, axis=-1, keepdims=True)
    vn = (vc * lax.rsqrt(var + LN_EPS) * lg_ref[...] + lb_ref[...]).astype(BF16)
    tril = lax.broadcasted_iota(jnp.int32, (CHUNK, CHUNK), 0) >= lax.broadcasted_iota(jnp.int32, (CHUNK, CHUNK), 1)
    for h in range(N_HEADS_C):
        c = slice(h * HEAD_DIM, (h + 1) * HEAD_DIM)
        wm = jnp.where(tril, w_ref[h], 0.0).astype(BF16)
        mix = jnp.dot(wm, vn[:, c], preferred_element_type=F32) + bs_ref[:, h:h + 1]
        o_ref[:, c] = (u[:, c] * mix).astype(o_ref.dtype)


def _gmlp_prompt(p, n, ln_g, ln_b, sgu_w, sgu_b):
    u_col = PC_COL // D_C
    vec = pl.BlockSpec((1, D_C), lambda r: (0, 0))
    return pl.pallas_call(
        _gmlp_body,
        grid=(n // CHUNK,),
        in_specs=[pl.BlockSpec((CHUNK, D_C), lambda r: (r, u_col)),
                  pl.BlockSpec((CHUNK, D_C), lambda r: (r, u_col + 1)),
                  vec, vec,
                  pl.BlockSpec((N_HEADS_C, CHUNK, CHUNK), lambda r: (0, 0, 0)),
                  pl.BlockSpec((CHUNK, N_HEADS_C), lambda r: (0, 0))],
        out_specs=pl.BlockSpec((CHUNK, D_C), lambda r: (r, 0)),
        out_shape=jax.ShapeDtypeStruct((n, D_C), BF16),
        compiler_params=_cparams(("parallel",)),
        name="gmlp",
    )(p, p, ln_g.reshape(1, D_C), ln_b.reshape(1, D_C), sgu_w, sgu_b.T)


def _mixer_prompt(p, bsz, t, cmp_pe, cmp_w, conv_w, conv_b, conv_ln_g, conv_ln_b, sgu_ln_g, sgu_ln_b, sgu_w, sgu_b):
    kc = _compress_prompt(p, bsz, t, cmp_pe, cmp_w.astype(BF16))
    o_a = _nsa_prompt(p, kc, bsz, t)
    o_b, tail = _conformer_prompt(p, bsz, t, conv_w, conv_b, conv_ln_g, conv_ln_b)
    o_c = _gmlp_prompt(p, bsz * t, sgu_ln_g, sgu_ln_b, sgu_w, sgu_b)
    cat = jnp.concatenate([o_a, o_b, o_c], axis=-1)
    p3 = p.reshape(bsz, t, D_IN_PAD)
    kv_row = (2, N_KV, HEAD_DIM)
    w_keep = min(WINDOW, t)
    new_state = (p3[:, :, KV_COL:KV_COL + 2 * KV_W].reshape((bsz, t) + kv_row),
                 p3[:, :, KV_COL + 2 * KV_W:KV_COL + 4 * KV_W].reshape((bsz, t) + kv_row),
                 p3[:, t - w_keep:, KV_COL + 4 * KV_W:KV_COL + 6 * KV_W].reshape((bsz, w_keep) + kv_row),
                 tail[:, CONV_HALO - (CONV_W - 1):])
    return cat, new_state


PAGE = 128
PAGES_PER_STEP = 16


def _pad_rows16(x):
    extra = (-x.shape[0]) % 16
    return x if extra == 0 else jnp.concatenate([x, jnp.zeros((extra,) + x.shape[1:], x.dtype)], axis=0)


def _page_specs(n, first, layer, per_step, n_pages):
    specs = []
    for r in range(n):
        for grp in range(2 * N_KV):
            def imap(b, j, pt, r=r, grp=grp):
                return (layer, pt[b, jnp.minimum(first(j) + r, n_pages - 1)], 0, grp)
            specs.append(pl.BlockSpec((None, None, PAGE, HEAD_DIM), imap))
    return specs


def _compress_paged_body(pt_ref, *refs, n_steps):
    np_ = PAGES_PER_STEP
    ng = 2 * N_KV
    pages, nxt, tail = refs[:ng * np_], refs[ng * np_:ng * (np_ + 1)], refs[ng * (np_ + 1):ng * (np_ + 2)]
    pe_ref, w_ref, o_ref, bm_s = refs[ng * (np_ + 2):]
    last = pl.program_id(1) == n_steps - 1
    per_page = PAGE // CMP_STRIDE
    n_tok = np_ * per_page
    t_new = tail[0].shape[0]
    for grp in range(ng):
        sel = grp // N_KV
        acc_a = jnp.zeros((n_tok, HEAD_DIM), F32)
        acc_b = jnp.zeros((n_tok + per_page, HEAD_DIM), F32)
        for s in range(CMP_STRIDE):
            x = jnp.concatenate([pages[r * ng + grp][pl.ds(s, per_page, stride=CMP_STRIDE), :] for r in range(np_)],
                                axis=0)
            if s < t_new:
                new_rows = jnp.broadcast_to(tail[grp][s:s + 1, :], (per_page, HEAD_DIM))
            else:
                new_rows = jnp.zeros((per_page, HEAD_DIM), F32)
            nx = jnp.where(last, new_rows, nxt[grp][pl.ds(s, per_page, stride=CMP_STRIDE), :])
            xa = (x + pe_ref[sel, s:s + 1, :]).astype(BF16)
            xb = (jnp.concatenate([x, nx], axis=0) + pe_ref[sel, CMP_STRIDE + s:CMP_STRIDE + s + 1, :]).astype(BF16)
            acc_a += jnp.dot(xa, w_ref[sel, s * HEAD_DIM:(s + 1) * HEAD_DIM, :], preferred_element_type=F32)
            acc_b += jnp.dot(xb, w_ref[sel, (CMP_STRIDE + s) * HEAD_DIM:(CMP_STRIDE + s + 1) * HEAD_DIM, :],
                             preferred_element_type=F32)
        bm_s[...] = acc_b
        o_ref[grp] = acc_a + bm_s[pl.ds(1, n_tok), :]


def _compress_paged(p, cache, page_table, layer, pe, w_bf):
    bsz, n_pages = page_table.shape
    t_new = p.shape[0] // bsz
    assert cache.shape[2] == PAGE and n_pages % PAGES_PER_STEP == 0 and t_new <= CMP_STRIDE
    n_steps = n_pages // PAGES_PER_STEP
    n_tok = PAGES_PER_STEP * PAGE // CMP_STRIDE
    ng = 2 * N_KV
    first = lambda j: j * PAGES_PER_STEP
    in_specs = (_page_specs(PAGES_PER_STEP, first, layer, PAGES_PER_STEP, n_pages)
                + _page_specs(1, lambda j: (j + 1) * PAGES_PER_STEP, layer, PAGES_PER_STEP, n_pages)
                + [pl.BlockSpec((t_new, HEAD_DIM), functools.partial(lambda b, j, pt, c: (b, c), c=KV_COL // HEAD_DIM + g))
                   for g in range(ng)]
                + [pl.BlockSpec((2, CMP_BLOCK, HEAD_DIM), lambda b, j, pt: (0, 0, 0)),
                   pl.BlockSpec((2, CMP_BLOCK * HEAD_DIM, HEAD_DIM), lambda b, j, pt: (0, 0, 0))])
    return pl.pallas_call(
        functools.partial(_compress_paged_body, n_steps=n_steps),
        grid_spec=pltpu.PrefetchScalarGridSpec(
            num_scalar_prefetch=1, grid=(bsz, n_steps), in_specs=in_specs,
            out_specs=pl.BlockSpec((None, ng, n_tok, HEAD_DIM), lambda b, j, pt: (b, 0, j, 0)),
            scratch_shapes=[pltpu.VMEM((n_tok + PAGE // CMP_STRIDE, HEAD_DIM), F32)]),
        out_shape=jax.ShapeDtypeStruct((bsz, ng, n_pages * PAGE // CMP_STRIDE, HEAD_DIM), F32),
        compiler_params=_cparams(("parallel", "arbitrary")),
        name="nsa_compress_paged",
    )(page_table, *([cache] * (ng * (PAGES_PER_STEP + 1))), *([p] * ng), pe, w_bf)


def _stack_heads(q_ref, kv):
    return jnp.concatenate([q_ref[:, (kv * GQA + g) * HEAD_DIM:(kv * GQA + g + 1) * HEAD_DIM]
                            for g in range(GQA)], axis=0).astype(BF16)


def _nsa_sample_cw_body(q_ref, kc_ref, wk0, wk1, wv0, wv1, new_ref, oc_ref, ow_ref, sel_ref, *, past, w_buf, ns):
    t = q_ref.shape[0]
    nc = kc_ref.shape[1]
    rows = GQA * t
    scale = HEAD_DIM ** -0.5
    n_sel = min(N_SLC, ns)
    wk = w_buf + LANES
    qpos = past + lax.broadcasted_iota(jnp.int32, (t, 1), 0)
    win_refs = ((wk0, wv0), (wk1, wv1))
    for kv in range(N_KV):
        qs = _stack_heads(q_ref, kv)
        kc = kc_ref[kv].astype(BF16)
        vc = kc_ref[N_KV + kv].astype(BF16)
        cidx = lax.broadcasted_iota(jnp.int32, (1, nc), 1)
        cmask = cidx * CMP_STRIDE + CMP_BLOCK - 1 <= qpos
        p_c = _softmax_rows((_dot_nt(qs, kc) * scale).reshape(GQA, t, nc), cmask)
        oc_ref[kv] = jnp.dot(p_c.reshape(rows, nc).astype(BF16), vc, preferred_element_type=F32)
        nl = sel_ref.shape[-1]
        score = _block_scores(_pad_rows16(jnp.sum(p_c, axis=0)), _pad_rows16(qpos), ns, nl)
        sel_ref[kv] = _topk_mask(score, ns, n_sel)[:t]

        pad = jnp.zeros((LANES - t, HEAD_DIM), F32)
        k_w = jnp.concatenate([win_refs[kv][0][...], new_ref[:, kv * HEAD_DIM:(kv + 1) * HEAD_DIM], pad],
                              axis=0).astype(BF16)
        v_w = jnp.concatenate([win_refs[kv][1][...], new_ref[:, (N_KV + kv) * HEAD_DIM:(N_KV + kv + 1) * HEAD_DIM],
                               pad], axis=0).astype(BF16)
        kpos = past - w_buf + lax.broadcasted_iota(jnp.int32, (1, wk), 1)
        wmask = (kpos <= qpos) & (kpos > qpos - WINDOW) & (kpos >= 0)
        p_w = _softmax_rows((_dot_nt(qs, k_w) * scale).reshape(GQA, t, wk), wmask)
        ow_ref[kv] = jnp.dot(p_w.reshape(rows, wk).astype(BF16), v_w, preferred_element_type=F32)


def _nsa_sample_cw(p, kc, cache_win, layer, bsz, past, ns, nl):
    t = p.shape[0] // bsz
    w_buf = cache_win.shape[2]
    nc = kc.shape[2]
    cw = 2 * N_KV * HEAD_DIM
    rows = GQA * t
    win_specs = [pl.BlockSpec((None, None, w_buf, HEAD_DIM), functools.partial(lambda b, g: (layer, b, 0, g), g=grp))
                 for grp in range(2 * N_KV)]
    o_spec = pl.BlockSpec((None, N_KV, rows, HEAD_DIM), lambda b: (b, 0, 0, 0))
    return pl.pallas_call(
        functools.partial(_nsa_sample_cw_body, past=past, w_buf=w_buf, ns=ns),
        grid=(bsz,),
        in_specs=[pl.BlockSpec((t, D_A), lambda b: (b, 0)),
                  pl.BlockSpec((None, 2 * N_KV, nc, HEAD_DIM), lambda b: (b, 0, 0, 0))]
                 + win_specs + [pl.BlockSpec((t, cw), lambda b: (b, KV_COL // cw + 2))],
        out_specs=[o_spec, o_spec, pl.BlockSpec((None, N_KV, t, nl), lambda b: (b, 0, 0, 0))],
        out_shape=[jax.ShapeDtypeStruct((bsz, N_KV, rows, HEAD_DIM), F32),
                   jax.ShapeDtypeStruct((bsz, N_KV, rows, HEAD_DIM), F32),
                   jax.ShapeDtypeStruct((bsz, N_KV, t, nl), F32)],
        compiler_params=_cparams(("parallel",)),
        name="nsa_sample_cmp_win",
    )(p, kc, cache_win, cache_win, cache_win, cache_win, p)


def _nsa_sample_slc_body(pt_ref, *refs, n_steps, past):
    np_ = PAGES_PER_STEP
    ng = 2 * N_KV
    q_ref, sel_ref = refs[:2]
    pages = refs[2:2 + ng * np_]
    new_ref, o_ref, m_s, l_s, acc_s = refs[2 + ng * np_:]
    j = pl.program_id(1)
    t = q_ref.shape[0]
    rows = GQA * t
    nl = sel_ref.shape[-1]
    scale = HEAD_DIM ** -0.5
    qpos = past + lax.broadcasted_iota(jnp.int32, (t, 1), 0)

    @pl.when(j == 0)
    def _():
        m_s[...] = jnp.full(m_s.shape, NEG_INF, F32)
        l_s[...] = jnp.zeros(l_s.shape, F32)
        acc_s[...] = jnp.zeros(acc_s.shape, F32)

    def update(kv, qs, k, v, k0):
        nk = k.shape[0]
        kpos = k0 + lax.broadcasted_iota(jnp.int32, (1, nk), 1)
        expand = jnp.where(lax.broadcasted_iota(jnp.int32, (nl, nk), 0) == kpos // SLC_BLOCK, 1.0, 0.0).astype(BF16)
        selk = jnp.dot(_pad_rows16(sel_ref[kv]).astype(BF16), expand, preferred_element_type=F32)[:t]
        allowed = (selk > 0.5) & (kpos <= qpos)
        s3 = jnp.where(allowed[None], (_dot_nt(qs, k) * scale).reshape(GQA, t, nk), NEG_INF)
        m_old = m_s[kv].reshape(GQA, t, 1)
        m_new = jnp.maximum(m_old, jnp.max(s3, axis=-1, keepdims=True))
        alpha = jnp.exp(m_old - m_new)
        p = jnp.exp(s3 - m_new) * allowed[None].astype(F32)
        l_s[kv] = (alpha * l_s[kv].reshape(GQA, t, 1) + jnp.sum(p, axis=-1, keepdims=True)).reshape(rows, 1)
        acc_s[kv] = (alpha.reshape(rows, 1) * acc_s[kv]
                     + jnp.dot(p.reshape(rows, nk).astype(BF16), v, preferred_element_type=F32))
        m_s[kv] = m_new.reshape(rows, 1)

    for kv in range(N_KV):
        qs = _stack_heads(q_ref, kv)
        k = jnp.concatenate([pages[r * ng + kv][...] for r in range(np_)], axis=0).astype(BF16)
        v = jnp.concatenate([pages[r * ng + N_KV + kv][...] for r in range(np_)], axis=0).astype(BF16)
        update(kv, qs, k, v, j * (np_ * PAGE))

        @pl.when(j == n_steps - 1)
        def _():
            pad = jnp.zeros((LANES - t, HEAD_DIM), F32)
            k_n = jnp.concatenate([new_ref[:, kv * HEAD_DIM:(kv + 1) * HEAD_DIM], pad], axis=0).astype(BF16)
            v_n = jnp.concatenate([new_ref[:, (N_KV + kv) * HEAD_DIM:(N_KV + kv + 1) * HEAD_DIM], pad],
                                  axis=0).astype(BF16)
            update(kv, qs, k_n, v_n, past)
            o_ref[kv] = acc_s[kv] / jnp.maximum(l_s[kv], 1e-30)


def _nsa_sample_slc(p, sel, cache, page_table, layer, past):
    bsz, n_pages = page_table.shape
    t = p.shape[0] // bsz
    nl = sel.shape[-1]
    n_steps = n_pages // PAGES_PER_STEP
    cw = 2 * N_KV * HEAD_DIM
    rows = GQA * t
    in_specs = ([pl.BlockSpec((t, D_A), lambda b, j, pt: (b, 0)),
                 pl.BlockSpec((None, N_KV, t, nl), lambda b, j, pt: (b, 0, 0, 0))]
                + _page_specs(PAGES_PER_STEP, lambda j: j * PAGES_PER_STEP, layer, PAGES_PER_STEP, n_pages)
                + [pl.BlockSpec((t, cw), lambda b, j, pt: (b, KV_COL // cw + 1))])
    return pl.pallas_call(
        functools.partial(_nsa_sample_slc_body, n_steps=n_steps, past=past),
        grid_spec=pltpu.PrefetchScalarGridSpec(
            num_scalar_prefetch=1, grid=(bsz, n_steps), in_specs=in_specs,
            out_specs=pl.BlockSpec((None, N_KV, rows, HEAD_DIM), lambda b, j, pt: (b, 0, 0, 0)),
            scratch_shapes=[pltpu.VMEM((N_KV, rows, 1), F32), pltpu.VMEM((N_KV, rows, 1), F32),
                            pltpu.VMEM((N_KV, rows, HEAD_DIM), F32)]),
        out_shape=jax.ShapeDtypeStruct((bsz, N_KV, rows, HEAD_DIM), F32),
        compiler_params=_cparams(("parallel", "arbitrary")),
        name="nsa_sample_slc",
    )(page_table, p, sel, *([cache] * (2 * N_KV * PAGES_PER_STEP)), p)


def _conformer_sample(p, state, bsz, conv_w, conv_b, ln_g, ln_b):
    t = p.shape[0] // bsz
    a_col = PB_COL // D_B
    hist = jnp.pad(state, ((0, 0), (CONV_HALO - (CONV_W - 1), 0), (0, 0)))
    wpad = jnp.pad(conv_w, ((0, CONV_HALO - CONV_W), (0, 0)))
    vec = pl.BlockSpec((1, D_B), lambda b, i: (0, 0))
    hspec = pl.BlockSpec((None, CONV_HALO, D_B), lambda b, i: (b, 0, 0))
    return pl.pallas_call(
        functools.partial(_conformer_body, tt=t, state_halo=True),
        grid=(bsz, 1),
        in_specs=[pl.BlockSpec((t, D_B), lambda b, i: (b, a_col)),
                  pl.BlockSpec((t, D_B), lambda b, i: (b, a_col + 1)),
                  hspec, hspec,
                  pl.BlockSpec((CONV_HALO, D_B), lambda b, i: (0, 0)), vec, vec, vec],
        out_specs=[pl.BlockSpec((t, D_B), lambda b, i: (b, 0)),
                   pl.BlockSpec((None, CONV_HALO, D_B), lambda b, i: (b, 0, 0))],
        out_shape=[jax.ShapeDtypeStruct((bsz * t, D_B), F32),
                   jax.ShapeDtypeStruct((bsz, CONV_HALO, D_B), F32)],
        scratch_shapes=[pltpu.VMEM((t + CONV_HALO, D_B), F32)],
        compiler_params=_cparams(("parallel", "arbitrary")),
        name="conformer_sample",
    )(p, p, hist, hist, wpad, conv_b.reshape(1, D_B), ln_g.reshape(1, D_B), ln_b.reshape(1, D_B))


def _gmlp_sample_body(u_ref, v_ref, lg_ref, lb_ref, w_ref, bs_ref, o_ref, vn_ref):
    t = u_ref.shape[0]
    u = jax.nn.gelu(u_ref[...])
    v = jax.nn.gelu(v_ref[...])
    mu = jnp.mean(v, axis=-1, keepdims=True)
    vc = v - mu
    var = jnp.mean(vc * vc, axis=-1, keepdims=True)
    vn = vc * lax.rsqrt(var + LN_EPS) * lg_ref[...] + lb_ref[...]
    vn_ref[...] = vn
    vb = vn.astype(BF16).astype(F32)
    tril = lax.broadcasted_iota(jnp.int32, (t, t), 0) >= lax.broadcasted_iota(jnp.int32, (t, t), 1)
    for h in range(N_HEADS_C):
        c = slice(h * HEAD_DIM, (h + 1) * HEAD_DIM)
        wm = jnp.where(tril, w_ref[h], 0.0).astype(BF16).astype(F32)
        mix = jnp.zeros((t, HEAD_DIM), F32)
        for s in range(t):
            mix = mix + wm[:, s:s + 1] * vb[s:s + 1, c]
        o_ref[:, c] = u[:, c] * (mix + bs_ref[:, h:h + 1])


def _gmlp_sample(p, bsz, ln_g, ln_b, sgu_w, sgu_b):
    t = p.shape[0] // bsz
    u_col = PC_COL // D_C
    vec = pl.BlockSpec((1, D_C), lambda b: (0, 0))
    row = pl.BlockSpec((t, D_C), lambda b: (b, 0))
    return pl.pallas_call(
        _gmlp_sample_body,
        grid=(bsz,),
        in_specs=[pl.BlockSpec((t, D_C), lambda b: (b, u_col)),
                  pl.BlockSpec((t, D_C), lambda b: (b, u_col + 1)),
                  vec, vec,
                  pl.BlockSpec((N_HEADS_C, t, t), lambda b: (0, 0, 0)),
                  pl.BlockSpec((t, N_HEADS_C), lambda b: (0, 0))],
        out_specs=[row, row],
        out_shape=[jax.ShapeDtypeStruct((bsz * t, D_C), F32), jax.ShapeDtypeStruct((bsz * t, D_C), F32)],
        compiler_params=_cparams(("parallel",)),
        name="gmlp_sample",
    )(p, p, ln_g.reshape(1, D_C), ln_b.reshape(1, D_C), sgu_w[:, :t, :t], sgu_b[:, :t].T)


def _mixer_sample(p, layer, cache_cmp, cache_slc, cache_win, state_conv, page_table, bsz,
                  cmp_pe, cmp_w, conv_w, conv_b, conv_ln_g, conv_ln_b, sgu_ln_g, sgu_ln_b, sgu_w, sgu_b):
    t = p.shape[0] // bsz
    n_pages = page_table.shape[1]
    past = n_pages * PAGE
    ns = -(-(past + t) // SLC_BLOCK)
    nl = -(-ns // LANES) * LANES
    kc = _compress_paged(p, cache_cmp, page_table, layer, cmp_pe, cmp_w.astype(BF16))
    o_cmp, o_win, sel = _nsa_sample_cw(p, kc, cache_win, layer, bsz, past, ns, nl)
    o_slc = _nsa_sample_slc(p, sel, cache_slc, page_table, layer, past)
    p3 = p.reshape(bsz, t, D_IN_PAD)
    gates = jax.nn.sigmoid(p3[:, :, GATE_COL:GATE_COL + N_BRANCH * N_HEADS_A]).reshape(bsz, t, N_BRANCH, N_HEADS_A)

    def heads(o):
        return o.reshape(bsz, N_KV, GQA, t, HEAD_DIM).transpose(0, 3, 1, 2, 4).reshape(bsz, t, N_HEADS_A, HEAD_DIM)

    o_a = (gates[:, :, 0, :, None] * heads(o_cmp) + gates[:, :, 1, :, None] * heads(o_slc)
           + gates[:, :, 2, :, None] * heads(o_win)).reshape(bsz * t, D_A)
    o_b, tail = _conformer_sample(p, state_conv[layer], bsz, conv_w, conv_b, conv_ln_g, conv_ln_b)
    o_c, vn = _gmlp_sample(p, bsz, sgu_ln_g, sgu_ln_b, sgu_w, sgu_b)
    cat = jnp.concatenate([o_a, o_b, o_c], axis=-1).astype(BF16)
    kv_row = (2, N_KV, HEAD_DIM)
    new_win = p3[:, :, KV_COL + 4 * KV_W:KV_COL + 6 * KV_W].reshape((bsz, t) + kv_row)
    new_state = (p3[:, :, KV_COL:KV_COL + 2 * KV_W].reshape((bsz, t) + kv_row),
                 p3[:, :, KV_COL + 2 * KV_W:KV_COL + 4 * KV_W].reshape((bsz, t) + kv_row),
                 jnp.concatenate([cache_win[layer][:, t:].reshape((bsz, -1) + kv_row), new_win], axis=1),
                 tail[:, CONV_HALO - (CONV_W - 1):],
                 vn.reshape(bsz, t, D_C))
    return cat, new_state


def _pad_w_in(w):
    k = w.shape[0]
    g = N_BRANCH * N_HEADS_A
    return jnp.concatenate([
        w[:, :GATE_COL + g], jnp.zeros((k, PB_COL - GATE_COL - g), w.dtype), w[:, GATE_COL + g:]],
        axis=1).astype(BF16)


def kernel(x_prompt, x_sample, cache_kv_cmp, cache_kv_slc, cache_kv_win, state_conv, page_table,
           c_prompt, c_sample, w_in, w_out, cmp_pe, cmp_w, conv_w, conv_b, conv_ln_g, conv_ln_b,
           sgu_ln_g, sgu_ln_b, sgu_w, sgu_b, ada_w, ada_b, ln_g, ln_b, router_w, router_b,
           moe_w1, moe_b1, moe_w2, moe_b2):
    bp, tp, d = x_prompt.shape
    bs, ts, _ = x_sample.shape
    n_p, n_s = bp * tp, bs * ts
    xp = x_prompt.reshape(n_p, d)
    xs = x_sample.reshape(n_s, d)
    c_all = jnp.concatenate([c_prompt, c_sample], axis=0)
    c_all = jax.nn.silu(jnp.pad(c_all, ((0, 16 - bp - bs), (0, 0))))
    st = [[] for _ in range(9)]
    flat = lambda c: c.reshape(c.shape[:3] + (2 * N_KV * HEAD_DIM,))
    cache_kv_cmp, cache_kv_slc, cache_kv_win = flat(cache_kv_cmp), flat(cache_kv_slc), flat(cache_kv_win)
    for l in range(DEPTH):
        mod = _matmul(c_all, ada_w[l], tm=16, tn=2048, tk=1024, bias=ada_b[l])
        mod_p = mod[:bp].reshape(bp, 1, 6, d)
        mod_s = jnp.repeat(mod[bp:bp + bs], ts, axis=0).reshape(1, n_s, 6, d)
        mp = [mod_p[:, :, i] for i in range(6)]
        ms = [mod_s[:, :, i] for i in range(6)]
        w_in_l = _pad_w_in(w_in[l])
        w_out_l = w_out[l].astype(BF16)
        mix_w = (cmp_pe[l], cmp_w[l], conv_w[l], conv_b[l], conv_ln_g[l], conv_ln_b[l],
                 sgu_ln_g[l], sgu_ln_b[l], sgu_w[l], sgu_b[l])

        pp = _matmul(xp, w_in_l, tm=1024, tn=1024, tk=512, mod=(mp[1], mp[0]))
        ps = _matmul(xs, w_in_l, tm=n_s, tn=1024, tk=512, mod=(ms[1], ms[0]))
        cat_p, st_p = _mixer_prompt(pp, bp, tp, *mix_w)
        cat_s, st_s = _mixer_sample(ps, l, cache_kv_cmp, cache_kv_slc, cache_kv_win, state_conv, page_table, bs, *mix_w)
        m_p = _matmul(cat_p, w_out_l, tm=1024, tn=1024, tk=512)
        m_s = _matmul(cat_s, w_out_l, tm=n_s, tn=1024, tk=512)
        xp, xmp = _deepnorm(xp, m_p, mp[2], ln_g[l, 0], ln_b[l, 0], mod=(mp[4], mp[3]))
        xs, xms = _deepnorm(xs, m_s, ms[2], ln_g[l, 0], ln_b[l, 0], mod=(ms[4], ms[3]), tm=n_s)
        f = _moe(jnp.concatenate([xmp, xms], axis=0), l, router_w[l], router_b[l], moe_w1, moe_b1, moe_w2, moe_b2)
        xp = _deepnorm(xp, f[:n_p], mp[5], ln_g[l, 1], ln_b[l, 1])
        xs = _deepnorm(xs, f[n_p:], ms[5], ln_g[l, 1], ln_b[l, 1], tm=n_s)
        for i in range(4):
            st[2 * i].append(st_p[i])
            st[2 * i + 1].append(st_s[i])
        st[8].append(st_s[4])
    return (xp.reshape(bp, tp, d), xs.reshape(bs, ts, d)) + tuple(jnp.stack(s) for s in st)
```

```python
import functools

import jax
import jax.numpy as jnp
import numpy as np
from jax import lax
from jax.experimental import pallas as pl
from jax.experimental.pallas import tpu as pltpu

F32 = jnp.float32
BF16 = jnp.bfloat16

D_MODEL = 4096
DEPTH = 2
HEAD_DIM = 128
D_A = D_MODEL // 2
D_B = D_MODEL // 4
D_C = D_MODEL // 4
D_MIX = D_A + D_B + D_C
N_HEADS_A = D_A // HEAD_DIM
N_KV = 2
GQA = N_HEADS_A // N_KV
KV_W = N_KV * HEAD_DIM
N_BRANCH = 3
N_HEADS_C = D_C // HEAD_DIM
CMP_BLOCK = 32
CMP_STRIDE = 16
SLC_BLOCK = 64
N_SLC = 16
WINDOW = 512
SEL_QBLOCK = 64
WIN_QBLOCK = 128
FORCE_BONUS = 1.0e4
CONV_W = 31
CHUNK = 128
N_EXPERTS = 32
TOP_K = 4
D_FF = D_MODEL // 2
SWIGLU_ALPHA = 1.702
SWIGLU_LIMIT = 7.0
DEEPNORM_ALPHA = (2 * DEPTH) ** 0.25
LN_EPS = 1e-5
NEG_INF = -1e30
SPLITS = (D_A, 6 * KV_W, N_BRANCH * N_HEADS_A, 2 * D_B, 2 * D_C)
D_IN = sum(SPLITS)

LANES = 128
V7X_VMEM_BYTES = 64 * 1024 * 1024
VMEM_LIMIT = 56 * 1024 * 1024

KV_COL = D_A
GATE_COL = D_A + 6 * KV_W
PB_COL = D_MODEL
PC_COL = PB_COL + 2 * D_B
D_IN_PAD = PC_COL + 2 * D_C

MOE_TM = 256


def _cparams(sem):
    return pltpu.CompilerParams(dimension_semantics=sem, vmem_limit_bytes=VMEM_LIMIT)


def _mm_body(*refs, n_k, modulate, has_bias):
    it = iter(refs)
    x_ref = next(it)
    sc_ref = next(it) if modulate else None
    sh_ref = next(it) if modulate else None
    w_ref = next(it)
    b_ref = next(it) if has_bias else None
    o_ref = next(it)
    acc_ref = next(it)
    k = pl.program_id(2)

    @pl.when(k == 0)
    def _():
        acc_ref[...] = jnp.zeros_like(acc_ref)

    x = x_ref[...]
    if modulate:
        x = x.astype(F32) * (1.0 + sc_ref[...]) + sh_ref[...]
    acc_ref[...] += jnp.dot(x.astype(BF16), w_ref[...].astype(BF16), preferred_element_type=F32)

    @pl.when(k == n_k - 1)
    def _():
        r = acc_ref[...]
        if has_bias:
            r = r + b_ref[...]
        o_ref[...] = r.astype(o_ref.dtype)


def _matmul(x, w, *, tm, tn, tk, out_dtype=F32, bias=None, mod=None, w_layer=None):
    m, kk = x.shape
    n = w.shape[-1]
    tm, tn, tk = min(tm, m), min(tn, n), min(tk, kk)
    assert m % tm == 0 and n % tn == 0 and kk % tk == 0
    n_k = kk // tk
    in_specs = [pl.BlockSpec((tm, tk), lambda i, j, k: (i, k))]
    args = [x]
    if mod is not None:
        g, r, _ = mod[0].shape
        rows = m // g
        assert rows % tm == 0 and r in (1, tm) and (r == 1 or rows == tm)
        per = rows // tm
        spec = pl.BlockSpec((None, r, tk), lambda i, j, k: (i // per, 0, k))
        in_specs += [spec, spec]
        args += [mod[0], mod[1]]
    if w_layer is None:
        in_specs.append(pl.BlockSpec((tk, tn), lambda i, j, k: (k, j)))
    else:
        in_specs.append(pl.BlockSpec((None, tk, tn), lambda i, j, k: (w_layer, k, j)))
    args.append(w)
    if bias is not None:
        in_specs.append(pl.BlockSpec((1, tn), lambda i, j, k: (0, j)))
        args.append(bias.reshape(1, n))
    return pl.pallas_call(
        functools.partial(_mm_body, n_k=n_k, modulate=mod is not None, has_bias=bias is not None),
        grid=(m // tm, n // tn, n_k),
        in_specs=in_specs,
        out_specs=pl.BlockSpec((tm, tn), lambda i, j, k: (i, j)),
        out_shape=jax.ShapeDtypeStruct((m, n), out_dtype),
        scratch_shapes=[pltpu.VMEM((tm, tn), F32)],
        compiler_params=_cparams(("parallel", "parallel", "arbitrary")),
        name="matmul",
    )(*args)


def _dn_body(*refs, modulate, n_terms):
    it = iter(refs)
    x_ref = next(it)
    m_refs = [next(it) for _ in range(n_terms)]
    gate_ref, g_ref, b_ref = (next(it) for _ in range(3))
    sc_ref = next(it) if modulate else None
    sh_ref = next(it) if modulate else None
    y_ref = next(it)
    ym_ref = next(it) if modulate else None
    m = m_refs[0][...]
    for m_ref in m_refs[1:]:
        m = m + m_ref[...]
    z = DEEPNORM_ALPHA * x_ref[...] + (1.0 + gate_ref[...]) * m
    mu = jnp.mean(z, axis=-1, keepdims=True)
    zc = z - mu
    var = jnp.mean(zc * zc, axis=-1, keepdims=True)
    y = zc * lax.rsqrt(var + LN_EPS) * g_ref[...] + b_ref[...]
    y_ref[...] = y
    if modulate:
        ym_ref[...] = (y * (1.0 + sc_ref[...]) + sh_ref[...]).astype(ym_ref.dtype)


def _deepnorm(x, m, gate, ln_g, ln_b, mod=None, *, tm=256):
    ms = list(m) if isinstance(m, (list, tuple)) else [m]
    n, d = x.shape
    tm = min(tm, n)
    assert n % tm == 0
    g, r, _ = gate.shape
    rows = n // g
    assert rows % tm == 0 and r in (1, tm) and (r == 1 or rows == tm)
    per = rows // tm
    row = pl.BlockSpec((tm, d), lambda i: (i, 0))
    grp = pl.BlockSpec((None, r, d), lambda i: (i // per, 0, 0))
    vec = pl.BlockSpec((1, d), lambda i: (0, 0))
    in_specs = [row] + [row] * len(ms) + [grp, vec, vec]
    args = [x] + ms + [gate, ln_g.reshape(1, d), ln_b.reshape(1, d)]
    out_specs = [row]
    out_shape = [jax.ShapeDtypeStruct((n, d), F32)]
    if mod is not None:
        in_specs += [grp, grp]
        args += [mod[0], mod[1]]
        out_specs.append(row)
        out_shape.append(jax.ShapeDtypeStruct((n, d), BF16))
    out = pl.pallas_call(
        functools.partial(_dn_body, modulate=mod is not None, n_terms=len(ms)),
        grid=(n // tm,),
        in_specs=in_specs,
        out_specs=out_specs,
        out_shape=out_shape,
        compiler_params=_cparams(("parallel",)),
        name="deepnorm",
    )(*args)
    return out if mod is not None else out[0]


def _new_expert(be_ref, i):
    return (i == 0) | (be_ref[i] != be_ref[jnp.maximum(i - 1, 0)])


def _gmm1_body(be_ref, nu_ref, x_ref, wg_ref, wl_ref, bg_ref, bl_ref, o_ref, wg_s, wl_s):
    i = pl.program_id(1)

    @pl.when(_new_expert(be_ref, i))
    def _():
        wg_s[...] = wg_ref[...].astype(BF16)
        wl_s[...] = wl_ref[...].astype(BF16)

    @pl.when(i < nu_ref[0])
    def _():
        x = x_ref[...]
        hg = jnp.dot(x, wg_s[...], preferred_element_type=F32) + bg_ref[...]
        hl = jnp.dot(x, wl_s[...], preferred_element_type=F32) + bl_ref[...]
        hg = jnp.minimum(hg, SWIGLU_LIMIT)
        hl = jnp.clip(hl, -SWIGLU_LIMIT, SWIGLU_LIMIT)
        act = hg * jax.nn.sigmoid(SWIGLU_ALPHA * hg) * (hl + 1.0)
        o_ref[...] = act.astype(o_ref.dtype)


def _gmm2_body(be_ref, nu_ref, a_ref, w_ref, b_ref, rw_ref, o_ref, w_s):
    i = pl.program_id(1)

    @pl.when(_new_expert(be_ref, i))
    def _():
        w_s[...] = w_ref[...].astype(BF16)

    @pl.when(i < nu_ref[0])
    def _():
        y = jnp.dot(a_ref[...], w_s[...], preferred_element_type=F32) + b_ref[...]
        o_ref[...] = y * rw_ref[...]


def _moe_experts(xs, row_w, blk_exp, n_used, layer, w1, b1, w2, b2, *, tn1=512, tn2=2048):
    cap, d = xs.shape
    tm = MOE_TM
    n_blk = cap // tm
    nj1 = D_FF // tn1
    nj2 = d // tn2
    b1r = b1.reshape(DEPTH, N_EXPERTS, 1, 2 * D_FF)
    b2r = b2.reshape(DEPTH, N_EXPERTS, 1, d)

    def row_blk(i, nu):
        return jnp.minimum(i, nu[0] - 1)

    act = pl.pallas_call(
        _gmm1_body,
        grid_spec=pltpu.PrefetchScalarGridSpec(
            num_scalar_prefetch=2,
            grid=(nj1, n_blk),
            in_specs=[
                pl.BlockSpec((tm, d), lambda j, i, be, nu: (row_blk(i, nu), 0)),
                pl.BlockSpec((None, None, d, tn1), lambda j, i, be, nu: (layer, be[i], 0, j)),
                pl.BlockSpec((None, None, d, tn1), lambda j, i, be, nu: (layer, be[i], 0, nj1 + j)),
                pl.BlockSpec((None, None, 1, tn1), lambda j, i, be, nu: (layer, be[i], 0, j)),
                pl.BlockSpec((None, None, 1, tn1), lambda j, i, be, nu: (layer, be[i], 0, nj1 + j)),
            ],
            out_specs=pl.BlockSpec((tm, tn1), lambda j, i, be, nu: (row_blk(i, nu), j)),
            scratch_shapes=[pltpu.VMEM((d, tn1), BF16), pltpu.VMEM((d, tn1), BF16)],
        ),
        out_shape=jax.ShapeDtypeStruct((cap, D_FF), BF16),
        compiler_params=_cparams(("arbitrary", "arbitrary")),
        name="moe_gmm1",
    )(blk_exp, n_used, xs, w1, w1, b1r, b1r)

    ys = pl.pallas_call(
        _gmm2_body,
        grid_spec=pltpu.PrefetchScalarGridSpec(
            num_scalar_prefetch=2,
            grid=(nj2, n_blk),
            in_specs=[
                pl.BlockSpec((tm, D_FF), lambda j, i, be, nu: (row_blk(i, nu), 0)),
                pl.BlockSpec((None, None, D_FF, tn2), lambda j, i, be, nu: (layer, be[i], 0, j)),
                pl.BlockSpec((None, None, 1, tn2), lambda j, i, be, nu: (layer, be[i], 0, j)),
                pl.BlockSpec((tm, 1), lambda j, i, be, nu: (row_blk(i, nu), 0)),
            ],
            out_specs=pl.BlockSpec((tm, tn2), lambda j, i, be, nu: (row_blk(i, nu), j)),
            scratch_shapes=[pltpu.VMEM((D_FF, tn2), BF16)],
        ),
        out_shape=jax.ShapeDtypeStruct((cap, d), F32),
        compiler_params=_cparams(("arbitrary", "arbitrary")),
        name="moe_gmm2",
    )(blk_exp, n_used, act, w2, b2r, row_w)
    return ys


def _route(logits):
    n = logits.shape[0]
    tm = MOE_TM
    cap = -(-(n * TOP_K + N_EXPERTS * (tm - 1)) // tm) * tm
    n_blk = cap // tm
    top_val, top_exp = lax.top_k(logits, TOP_K)
    weights = jax.nn.softmax(top_val, axis=-1)
    onehot = (top_exp[:, :, None] == jnp.arange(N_EXPERTS, dtype=jnp.int32)).any(axis=1).astype(jnp.int32)
    counts = onehot.sum(axis=0)
    padded = (counts + tm - 1) // tm * tm
    pad_end = jnp.cumsum(padded)
    pad_start = pad_end - padded
    rank = jnp.cumsum(onehot, axis=0) - onehot
    pos = pad_start[top_exp] + jnp.take_along_axis(rank, top_exp, axis=1)
    tok = jnp.broadcast_to(jnp.arange(n, dtype=jnp.int32)[:, None], (n, TOP_K))
    row_tok = jnp.zeros((cap,), jnp.int32).at[pos.reshape(-1)].set(tok.reshape(-1))
    row_w = jnp.zeros((cap,), F32).at[pos.reshape(-1)].set(weights.reshape(-1))
    blk_row = jnp.arange(n_blk, dtype=jnp.int32)[:, None] * tm
    blk_exp = jnp.minimum(jnp.sum((pad_end[None, :] <= blk_row).astype(jnp.int32), axis=1), N_EXPERTS - 1)
    n_used = (pad_end[-1:] // tm).astype(jnp.int32)
    return row_tok, row_w.reshape(cap, 1), blk_exp, n_used, pos


def _moe(xm, layer, router_w, router_b, w1, b1, w2, b2):
    n, d = xm.shape
    rw = jnp.pad(router_w, ((0, 0), (0, LANES - N_EXPERTS))).astype(BF16)
    rb = jnp.pad(router_b, (0, LANES - N_EXPERTS))
    tm = next((c for c in range(min(n, 4096) // 16 * 16, 0, -16) if n % c == 0), n)
    logits = _matmul(xm, rw, tm=tm, tn=LANES, tk=1024, bias=rb)[:, :N_EXPERTS]
    row_tok, row_w, blk_exp, n_used, pos = _route(logits)
    xs = jnp.take(xm, row_tok, axis=0, mode="clip")
    ys = _moe_experts(xs, row_w, blk_exp, n_used, layer, w1, b1, w2, b2)
    return ys, pos


def _expert_rows(ys, pos):
    return [jnp.take(ys, pos[:, k], axis=0, mode="clip") for k in range(TOP_K)]


def _dot_nt(a, b):
    return lax.dot_general(a, b, (((1,), (1,)), ((), ())), preferred_element_type=F32)


def _compress_body(*refs, nch):
    x_refs, (pe_ref, w_ref, o_ref) = refs[:2 * N_KV], refs[2 * N_KV:]
    for grp in range(2 * N_KV):
        sel = grp // N_KV
        acc_a = jnp.zeros((nch, HEAD_DIM), F32)
        acc_b = jnp.zeros((nch, HEAD_DIM), F32)
        for s in range(CMP_STRIDE):
            x = x_refs[grp][pl.ds(s, nch, stride=CMP_STRIDE), :]
            xa = (x + pe_ref[sel, s:s + 1, :]).astype(BF16)
            xb = (x + pe_ref[sel, CMP_STRIDE + s:CMP_STRIDE + s + 1, :]).astype(BF16)
            acc_a += jnp.dot(xa, w_ref[sel, s * HEAD_DIM:(s + 1) * HEAD_DIM, :], preferred_element_type=F32)
            acc_b += jnp.dot(xb, w_ref[sel, (CMP_STRIDE + s) * HEAD_DIM:(CMP_STRIDE + s + 1) * HEAD_DIM, :],
                             preferred_element_type=F32)
        o_ref[grp] = acc_a + pltpu.roll(acc_b, nch - 1, axis=0)


def _compress_prompt(p, bsz, t, pe, w_bf):
    nch = t // CMP_STRIDE
    x_specs = [pl.BlockSpec((t, HEAD_DIM), functools.partial(lambda b, c: (b, c), c=KV_COL // HEAD_DIM + grp))
               for grp in range(2 * N_KV)]
    return pl.pallas_call(
        functools.partial(_compress_body, nch=nch),
        grid=(bsz,),
        in_specs=x_specs + [pl.BlockSpec((2, CMP_BLOCK, HEAD_DIM), lambda b: (0, 0, 0)),
                            pl.BlockSpec((2, CMP_BLOCK * HEAD_DIM, HEAD_DIM), lambda b: (0, 0, 0))],
        out_specs=pl.BlockSpec((None, 2 * N_KV, nch, HEAD_DIM), lambda b: (b, 0, 0, 0)),
        out_shape=jax.ShapeDtypeStruct((bsz, 2 * N_KV, nch, HEAD_DIM), F32),
        compiler_params=_cparams(("parallel",)),
        name="nsa_compress",
    )(*([p] * (2 * N_KV)), pe, w_bf)


def _softmax_rows(s3, mask):
    s3 = jnp.where(mask[None], s3, NEG_INF)
    m = jnp.max(s3, axis=-1, keepdims=True)
    p = jnp.exp(s3 - m) * mask[None].astype(F32)
    return p / jnp.maximum(jnp.sum(p, axis=-1, keepdims=True), 1e-30)


def _topk_mask(score, ns, n_sel):
    lane = lax.broadcasted_iota(jnp.int32, score.shape, 1)
    cnt = jnp.zeros(score.shape, F32)
    for i in range(ns):
        c = score[:, i:i + 1]
        beats = (c > score) | ((c == score) & (lane > i))
        cnt = cnt + jnp.where(beats, 1.0, 0.0)
    return jnp.where((cnt < n_sel) & (lane < ns), 1.0, 0.0)


def _block_scores(psum, qpos, ns, nblk_lanes):
    nc = psum.shape[1]
    ci = lax.broadcasted_iota(jnp.int32, (nc, nblk_lanes), 0) * CMP_STRIDE
    bj = lax.broadcasted_iota(jnp.int32, (nc, nblk_lanes), 1) * SLC_BLOCK
    cover = jnp.where((ci < bj + SLC_BLOCK) & (ci + CMP_BLOCK > bj), 1.0, 0.0).astype(BF16)
    hi = psum.astype(BF16)
    lo = (psum - hi.astype(F32)).astype(BF16)
    imp = jnp.dot(hi, cover, preferred_element_type=F32) + jnp.dot(lo, cover, preferred_element_type=F32)
    blk = lax.broadcasted_iota(jnp.int32, (1, nblk_lanes), 1)
    cur = qpos // SLC_BLOCK
    forced = (blk == 0) | (blk == cur) | (blk == cur - 1)
    score = jnp.where(blk <= cur, imp + jnp.where(forced, FORCE_BONUS, 0.0), NEG_INF)
    return jnp.where(blk < ns, score, -3.0e38)


def _nsa_prompt_body(q_ref, g_ref, kc_ref, slc_ref, win_ref, o_ref, slc_s, win_s, m_s, l_s, acc_s,
                     *, t, tq, tk, wk):
    i = pl.program_id(1)
    nch = t // CMP_STRIDE
    nc = nch - CMP_BLOCK // CMP_STRIDE + 1
    ns = t // SLC_BLOCK
    n_sel = min(N_SLC, ns)
    scale = HEAD_DIM ** -0.5
    rows = GQA * tq

    @pl.when(i == 0)
    def _():
        slc_s[...] = slc_ref[...].astype(BF16)
        win_s[...] = win_ref[...].astype(BF16)

    q0 = i * tq
    qpos = q0 + lax.broadcasted_iota(jnp.int32, (tq, 1), 0)
    gates = jax.nn.sigmoid(g_ref[...])
    for kv in range(N_KV):
        qs = jnp.concatenate([q_ref[:, (kv * GQA + g) * HEAD_DIM:(kv * GQA + g + 1) * HEAD_DIM]
                              for g in range(GQA)], axis=0).astype(BF16)
        kc = kc_ref[kv].astype(BF16)
        vc = kc_ref[N_KV + kv].astype(BF16)
        cidx = lax.broadcasted_iota(jnp.int32, (1, nch), 1)
        cmask = (cidx * CMP_STRIDE + CMP_BLOCK - 1 <= qpos) & (cidx < nc)
        p_c = _softmax_rows((_dot_nt(qs, kc) * scale).reshape(GQA, tq, nch), cmask)
        o_c = jnp.dot(p_c.reshape(rows, nch).astype(BF16), vc, preferred_element_type=F32)
        sel = _topk_mask(_block_scores(jnp.sum(p_c, axis=0), qpos, ns, LANES), ns, n_sel).astype(BF16)

        m_s[...] = jnp.full(m_s.shape, NEG_INF, F32)
        l_s[...] = jnp.zeros(l_s.shape, F32)
        acc_s[...] = jnp.zeros(acc_s.shape, F32)

        def chunk(c, carry):
            k0 = pl.multiple_of(c * tk, tk)
            k_c = slc_s[pl.ds(k0, tk), kv * HEAD_DIM:(kv + 1) * HEAD_DIM]
            v_c = slc_s[pl.ds(k0, tk), (N_KV + kv) * HEAD_DIM:(N_KV + kv + 1) * HEAD_DIM]
            kpos = k0 + lax.broadcasted_iota(jnp.int32, (1, tk), 1)
            expand = jnp.where(lax.broadcasted_iota(jnp.int32, (LANES, tk), 0) == kpos // SLC_BLOCK,
                               1.0, 0.0).astype(BF16)
            allowed = (jnp.dot(sel, expand, preferred_element_type=F32) > 0.5) & (kpos <= qpos)
            s3 = jnp.where(allowed[None], (_dot_nt(qs, k_c) * scale).reshape(GQA, tq, tk), NEG_INF)
            m_old = m_s[...].reshape(GQA, tq, 1)
            m_new = jnp.maximum(m_old, jnp.max(s3, axis=-1, keepdims=True))
            alpha = jnp.exp(m_old - m_new)
            p = jnp.exp(s3 - m_new) * allowed[None].astype(F32)
            l_s[...] = (alpha * l_s[...].reshape(GQA, tq, 1) + jnp.sum(p, axis=-1, keepdims=True)).reshape(rows, 1)
            acc_s[...] = (alpha.reshape(rows, 1) * acc_s[...]
                          + jnp.dot(p.reshape(rows, tk).astype(BF16), v_c, preferred_element_type=F32))
            m_s[...] = m_new.reshape(rows, 1)
            return carry

        lax.fori_loop(0, ((i + 1) * tq + tk - 1) // tk, chunk, 0)
        o_s = acc_s[...] / jnp.maximum(l_s[...], 1e-30)

        k0 = pl.multiple_of(jnp.clip(q0 - WINDOW, 0, t - wk), tq)
        k_w = win_s[pl.ds(k0, wk), kv * HEAD_DIM:(kv + 1) * HEAD_DIM]
        v_w = win_s[pl.ds(k0, wk), (N_KV + kv) * HEAD_DIM:(N_KV + kv + 1) * HEAD_DIM]
        kpos = k0 + lax.broadcasted_iota(jnp.int32, (1, wk), 1)
        wmask = (kpos <= qpos) & (kpos > qpos - WINDOW)
        p_w = _softmax_rows((_dot_nt(qs, k_w) * scale).reshape(GQA, tq, wk), wmask)
        o_w = jnp.dot(p_w.reshape(rows, wk).astype(BF16), v_w, preferred_element_type=F32)

        for g in range(GQA):
            h = kv * GQA + g
            r = slice(g * tq, (g + 1) * tq)
            o = (gates[:, h:h + 1] * o_c[r] + gates[:, N_HEADS_A + h:N_HEADS_A + h + 1] * o_s[r]
                 + gates[:, 2 * N_HEADS_A + h:2 * N_HEADS_A + h + 1] * o_w[r])
            o_ref[:, h * HEAD_DIM:(h + 1) * HEAD_DIM] = o.astype(o_ref.dtype)


def _nsa_prompt(p, kc, bsz, t, *, tq=128, tk=256):
    tq, tk = min(tq, t), min(tk, t)
    wk = min(t, WINDOW + tq)
    nq = t // tq
    nch = t // CMP_STRIDE
    cw = 2 * N_KV * HEAD_DIM
    rows = GQA * tq
    return pl.pallas_call(
        functools.partial(_nsa_prompt_body, t=t, tq=tq, tk=tk, wk=wk),
        grid=(bsz, nq),
        in_specs=[pl.BlockSpec((tq, D_A), lambda b, i: (b * nq + i, 0)),
                  pl.BlockSpec((tq, LANES), lambda b, i: (b * nq + i, GATE_COL // LANES)),
                  pl.BlockSpec((None, 2 * N_KV, nch, HEAD_DIM), lambda b, i: (b, 0, 0, 0)),
                  pl.BlockSpec((t, cw), lambda b, i: (b, KV_COL // cw + 1)),
                  pl.BlockSpec((t, cw), lambda b, i: (b, KV_COL // cw + 2))],
        out_specs=pl.BlockSpec((tq, D_A), lambda b, i: (b * nq + i, 0)),
        out_shape=jax.ShapeDtypeStruct((bsz * t, D_A), BF16),
        scratch_shapes=[pltpu.VMEM((t, cw), BF16), pltpu.VMEM((t, cw), BF16),
                        pltpu.VMEM((rows, 1), F32), pltpu.VMEM((rows, 1), F32), pltpu.VMEM((rows, HEAD_DIM), F32)],
        compiler_params=_cparams(("parallel", "arbitrary")),
        name="nsa_prompt",
    )(p, p, kc, p, p)


CONV_HALO = 32


def _conformer_body(a_ref, g_ref, ah_ref, gh_ref, w_ref, cb_ref, lg_ref, lb_ref, o_ref, tail_ref, ext_s,
                    *, tt, state_halo):
    glu = a_ref[...] * jax.nn.sigmoid(g_ref[...])
    if state_halo:
        ext_s[0:CONV_HALO, :] = ah_ref[...]
    else:
        halo = ah_ref[...] * jax.nn.sigmoid(gh_ref[...])
        ext_s[0:CONV_HALO, :] = jnp.where(pl.program_id(1) > 0, halo, 0.0)
    ext_s[CONV_HALO:, :] = glu
    y = jnp.zeros((tt, D_B), F32)
    for j in range(CONV_W):
        y = y + ext_s[pl.ds(CONV_HALO - (CONV_W - 1) + j, tt), :] * w_ref[j:j + 1, :]
    y = y + cb_ref[...]
    mu = jnp.mean(y, axis=-1, keepdims=True)
    yc = y - mu
    var = jnp.mean(yc * yc, axis=-1, keepdims=True)
    z = yc * lax.rsqrt(var + LN_EPS) * lg_ref[...] + lb_ref[...]
    o_ref[...] = (z * jax.nn.sigmoid(z)).astype(o_ref.dtype)
    tail_ref[...] = ext_s[tt:tt + CONV_HALO, :]


def _conformer_prompt(p, bsz, t, conv_w, conv_b, ln_g, ln_b, *, tt=256):
    tt = min(tt, t)
    nt = t // tt
    a_col = PB_COL // D_B
    per = tt // CONV_HALO
    wpad = jnp.pad(conv_w, ((0, CONV_HALO - CONV_W), (0, 0)))
    vec = pl.BlockSpec((1, D_B), lambda b, i: (0, 0))

    def halo_row(b, i):
        return jnp.maximum((b * nt + i) * per - 1, 0)

    return pl.pallas_call(
        functools.partial(_conformer_body, tt=tt, state_halo=False),
        grid=(bsz, nt),
        in_specs=[pl.BlockSpec((tt, D_B), lambda b, i: (b * nt + i, a_col)),
                  pl.BlockSpec((tt, D_B), lambda b, i: (b * nt + i, a_col + 1)),
                  pl.BlockSpec((CONV_HALO, D_B), lambda b, i: (halo_row(b, i), a_col)),
                  pl.BlockSpec((CONV_HALO, D_B), lambda b, i: (halo_row(b, i), a_col + 1)),
                  pl.BlockSpec((CONV_HALO, D_B), lambda b, i: (0, 0)), vec, vec, vec],
        out_specs=[pl.BlockSpec((tt, D_B), lambda b, i: (b * nt + i, 0)),
                   pl.BlockSpec((None, CONV_HALO, D_B), lambda b, i: (b, 0, 0))],
        out_shape=[jax.ShapeDtypeStruct((bsz * t, D_B), BF16),
                   jax.ShapeDtypeStruct((bsz, CONV_HALO, D_B), F32)],
        scratch_shapes=[pltpu.VMEM((tt + CONV_HALO, D_B), F32)],
        compiler_params=_cparams(("parallel", "arbitrary")),
        name="conformer",
    )(p, p, p, p, wpad, conv_b.reshape(1, D_B), ln_g.reshape(1, D_B), ln_b.reshape(1, D_B))


def _gmlp_body(u_ref, v_ref, lg_ref, lb_ref, w_ref, bs_ref, o_ref):
    u = jax.nn.gelu(u_ref[...])
    v = jax.nn.gelu(v_ref[...])
    mu = jnp.mean(v, axis=-1, keepdims=True)
    vc = v - mu
    var = jnp.mean(vc * vc, axis=-1, keepdims=True)
    vn = (vc * lax.rsqrt(var + LN_EPS) * lg_ref[...] + lb_ref[...]).astype(BF16)
    tril = lax.broadcasted_iota(jnp.int32, (CHUNK, CHUNK), 0) >= lax.broadcasted_iota(jnp.int32, (CHUNK, CHUNK), 1)
    for h in range(N_HEADS_C):
        c = slice(h * HEAD_DIM, (h + 1) * HEAD_DIM)
        wm = jnp.where(tril, w_ref[h], 0.0).astype(BF16)
        mix = jnp.dot(wm, vn[:, c], preferred_element_type=F32) + bs_ref[:, h:h + 1]
        o_ref[:, c] = (u[:, c] * mix).astype(o_ref.dtype)


def _gmlp_prompt(p, n, ln_g, ln_b, sgu_w, sgu_b):
    u_col = PC_COL // D_C
    vec = pl.BlockSpec((1, D_C), lambda r: (0, 0))
    return pl.pallas_call(
        _gmlp_body,
        grid=(n // CHUNK,),
        in_specs=[pl.BlockSpec((CHUNK, D_C), lambda r: (r, u_col)),
                  pl.BlockSpec((CHUNK, D_C), lambda r: (r, u_col + 1)),
                  vec, vec,
                  pl.BlockSpec((N_HEADS_C, CHUNK, CHUNK), lambda r: (0, 0, 0)),
                  pl.BlockSpec((CHUNK, N_HEADS_C), lambda r: (0, 0))],
        out_specs=pl.BlockSpec((CHUNK, D_C), lambda r: (r, 0)),
        out_shape=jax.ShapeDtypeStruct((n, D_C), BF16),
        compiler_params=_cparams(("parallel",)),
        name="gmlp",
    )(p, p, ln_g.reshape(1, D_C), ln_b.reshape(1, D_C), sgu_w, sgu_b.T)


def _mixer_prompt(p, bsz, t, cmp_pe, cmp_w, conv_w, conv_b, conv_ln_g, conv_ln_b, sgu_ln_g, sgu_ln_b, sgu_w, sgu_b):
    kc = _compress_prompt(p, bsz, t, cmp_pe, cmp_w.astype(BF16))
    o_a = _nsa_prompt(p, kc, bsz, t)
    o_b, tail = _conformer_prompt(p, bsz, t, conv_w, conv_b, conv_ln_g, conv_ln_b)
    o_c = _gmlp_prompt(p, bsz * t, sgu_ln_g, sgu_ln_b, sgu_w, sgu_b)
    cat = jnp.concatenate([o_a, o_b, o_c], axis=-1)
    p3 = p.reshape(bsz, t, D_IN_PAD)
    kv_row = (2, N_KV, HEAD_DIM)
    w_keep = min(WINDOW, t)
    new_state = (p3[:, :, KV_COL:KV_COL + 2 * KV_W].reshape((bsz, t) + kv_row),
                 p3[:, :, KV_COL + 2 * KV_W:KV_COL + 4 * KV_W].reshape((bsz, t) + kv_row),
                 p3[:, t - w_keep:, KV_COL + 4 * KV_W:KV_COL + 6 * KV_W].reshape((bsz, w_keep) + kv_row),
                 tail[:, CONV_HALO - (CONV_W - 1):])
    return cat, new_state


PAGE = 128
PAGES_PER_STEP = 16


def _pad_rows16(x):
    extra = (-x.shape[0]) % 16
    return x if extra == 0 else jnp.concatenate([x, jnp.zeros((extra,) + x.shape[1:], x.dtype)], axis=0)


def _page_specs(n, first, layer, per_step, n_pages):
    specs = []
    for r in range(n):
        for sel in range(2):
            def imap(b, j, pt, r=r, sel=sel):
                return (layer, pt[b, jnp.minimum(first(j) + r, n_pages - 1)], 0, sel, 0, 0)
            specs.append(pl.BlockSpec((None, None, PAGE, None, N_KV, HEAD_DIM), imap))
    return specs


def _compress_paged_body(pt_ref, *refs, n_steps):
    np_ = PAGES_PER_STEP
    ng = 2 * N_KV
    pages, nxt, tail = refs[:2 * np_], refs[2 * np_:2 * np_ + 2], refs[2 * np_ + 2:2 * np_ + 2 + ng]
    pe_ref, w_ref, o_ref, bm_s = refs[2 * np_ + 2 + ng:]
    last = pl.program_id(1) == n_steps - 1
    per_page = PAGE // CMP_STRIDE
    n_tok = np_ * per_page
    t_new = tail[0].shape[0]
    for grp in range(ng):
        sel, kvh = grp // N_KV, grp % N_KV
        acc_a = jnp.zeros((n_tok, HEAD_DIM), F32)
        acc_b = jnp.zeros((n_tok + per_page, HEAD_DIM), F32)
        for s in range(CMP_STRIDE):
            x = jnp.concatenate([pages[r * 2 + sel][pl.ds(s, per_page, stride=CMP_STRIDE), kvh, :]
                                 for r in range(np_)], axis=0)
            if s < t_new:
                new_rows = jnp.broadcast_to(tail[grp][s:s + 1, :], (per_page, HEAD_DIM))
            else:
                new_rows = jnp.zeros((per_page, HEAD_DIM), F32)
            nx = jnp.where(last, new_rows, nxt[sel][pl.ds(s, per_page, stride=CMP_STRIDE), kvh, :])
            xa = (x + pe_ref[sel, s:s + 1, :]).astype(BF16)
            xb = (jnp.concatenate([x, nx], axis=0) + pe_ref[sel, CMP_STRIDE + s:CMP_STRIDE + s + 1, :]).astype(BF16)
            acc_a += jnp.dot(xa, w_ref[sel, s * HEAD_DIM:(s + 1) * HEAD_DIM, :], preferred_element_type=F32)
            acc_b += jnp.dot(xb, w_ref[sel, (CMP_STRIDE + s) * HEAD_DIM:(CMP_STRIDE + s + 1) * HEAD_DIM, :],
                             preferred_element_type=F32)
        bm_s[...] = acc_b
        o_ref[grp] = acc_a + bm_s[pl.ds(1, n_tok), :]


def _compress_paged(p, cache, page_table, layer, pe, w_bf):
    bsz, n_pages = page_table.shape
    t_new = p.shape[0] // bsz
    assert cache.shape[2:] == (PAGE, 2, N_KV, HEAD_DIM) and n_pages % PAGES_PER_STEP == 0 and t_new <= CMP_STRIDE
    n_steps = n_pages // PAGES_PER_STEP
    n_tok = PAGES_PER_STEP * PAGE // CMP_STRIDE
    ng = 2 * N_KV
    first = lambda j: j * PAGES_PER_STEP
    in_specs = (_page_specs(PAGES_PER_STEP, first, layer, PAGES_PER_STEP, n_pages)
                + _page_specs(1, lambda j: (j + 1) * PAGES_PER_STEP, layer, PAGES_PER_STEP, n_pages)
                + [pl.BlockSpec((t_new, HEAD_DIM), functools.partial(lambda b, j, pt, c: (b, c), c=KV_COL // HEAD_DIM + g))
                   for g in range(ng)]
                + [pl.BlockSpec((2, CMP_BLOCK, HEAD_DIM), lambda b, j, pt: (0, 0, 0)),
                   pl.BlockSpec((2, CMP_BLOCK * HEAD_DIM, HEAD_DIM), lambda b, j, pt: (0, 0, 0))])
    return pl.pallas_call(
        functools.partial(_compress_paged_body, n_steps=n_steps),
        grid_spec=pltpu.PrefetchScalarGridSpec(
            num_scalar_prefetch=1, grid=(bsz, n_steps), in_specs=in_specs,
            out_specs=pl.BlockSpec((None, ng, n_tok, HEAD_DIM), lambda b, j, pt: (b, 0, j, 0)),
            scratch_shapes=[pltpu.VMEM((n_tok + PAGE // CMP_STRIDE, HEAD_DIM), F32)]),
        out_shape=jax.ShapeDtypeStruct((bsz, ng, n_pages * PAGE // CMP_STRIDE, HEAD_DIM), F32),
        compiler_params=_cparams(("parallel", "arbitrary")),
        name="nsa_compress_paged",
    )(page_table, *([cache] * (2 * (PAGES_PER_STEP + 1))), *([p] * ng), pe, w_bf)


def _stack_heads(q_ref, kv):
    return jnp.concatenate([q_ref[:, (kv * GQA + g) * HEAD_DIM:(kv * GQA + g + 1) * HEAD_DIM]
                            for g in range(GQA)], axis=0).astype(BF16)


def _nsa_sample_cw_body(q_ref, kc_ref, wk_ref, wv_ref, new_ref, oc_ref, ow_ref, sel_ref, *, past, w_buf, ns):
    t = q_ref.shape[0]
    nc = kc_ref.shape[1]
    rows = GQA * t
    scale = HEAD_DIM ** -0.5
    n_sel = min(N_SLC, ns)
    wk = w_buf + LANES
    qpos = past + lax.broadcasted_iota(jnp.int32, (t, 1), 0)
    for kv in range(N_KV):
        qs = _stack_heads(q_ref, kv)
        kc = kc_ref[kv].astype(BF16)
        vc = kc_ref[N_KV + kv].astype(BF16)
        cidx = lax.broadcasted_iota(jnp.int32, (1, nc), 1)
        cmask = cidx * CMP_STRIDE + CMP_BLOCK - 1 <= qpos
        p_c = _softmax_rows((_dot_nt(qs, kc) * scale).reshape(GQA, t, nc), cmask)
        oc_ref[kv] = jnp.dot(p_c.reshape(rows, nc).astype(BF16), vc, preferred_element_type=F32)
        nl = sel_ref.shape[-1]
        score = _block_scores(_pad_rows16(jnp.sum(p_c, axis=0)), _pad_rows16(qpos), ns, nl)
        sel_ref[kv] = _topk_mask(score, ns, n_sel)[:t]

        pad = jnp.zeros((LANES - t, HEAD_DIM), F32)
        k_w = jnp.concatenate([wk_ref[:, kv, :], new_ref[:, kv * HEAD_DIM:(kv + 1) * HEAD_DIM], pad],
                              axis=0).astype(BF16)
        v_w = jnp.concatenate([wv_ref[:, kv, :], new_ref[:, (N_KV + kv) * HEAD_DIM:(N_KV + kv + 1) * HEAD_DIM],
                               pad], axis=0).astype(BF16)
        kpos = past - w_buf + lax.broadcasted_iota(jnp.int32, (1, wk), 1)
        wmask = (kpos <= qpos) & (kpos > qpos - WINDOW) & (kpos >= 0)
        p_w = _softmax_rows((_dot_nt(qs, k_w) * scale).reshape(GQA, t, wk), wmask)
        ow_ref[kv] = jnp.dot(p_w.reshape(rows, wk).astype(BF16), v_w, preferred_element_type=F32)


def _nsa_sample_cw(p, kc, cache_win, layer, bsz, past, ns, nl):
    t = p.shape[0] // bsz
    w_buf = cache_win.shape[2]
    nc = kc.shape[2]
    cw = 2 * N_KV * HEAD_DIM
    rows = GQA * t
    win_specs = [pl.BlockSpec((None, None, w_buf, None, N_KV, HEAD_DIM),
                              functools.partial(lambda b, s: (layer, b, 0, s, 0, 0), s=sel)) for sel in range(2)]
    o_spec = pl.BlockSpec((None, N_KV, rows, HEAD_DIM), lambda b: (b, 0, 0, 0))
    return pl.pallas_call(
        functools.partial(_nsa_sample_cw_body, past=past, w_buf=w_buf, ns=ns),
        grid=(bsz,),
        in_specs=[pl.BlockSpec((t, D_A), lambda b: (b, 0)),
                  pl.BlockSpec((None, 2 * N_KV, nc, HEAD_DIM), lambda b: (b, 0, 0, 0))]
                 + win_specs + [pl.BlockSpec((t, cw), lambda b: (b, KV_COL // cw + 2))],
        out_specs=[o_spec, o_spec, pl.BlockSpec((None, N_KV, t, nl), lambda b: (b, 0, 0, 0))],
        out_shape=[jax.ShapeDtypeStruct((bsz, N_KV, rows, HEAD_DIM), F32),
                   jax.ShapeDtypeStruct((bsz, N_KV, rows, HEAD_DIM), F32),
                   jax.ShapeDtypeStruct((bsz, N_KV, t, nl), F32)],
        compiler_params=_cparams(("parallel",)),
        name="nsa_sample_cmp_win",
    )(p, kc, cache_win, cache_win, p)


def _nsa_sample_slc_body(pt_ref, *refs, n_steps, past):
    np_ = PAGES_PER_STEP
    ng = 2 * N_KV
    q_ref, sel_ref = refs[:2]
    pages = refs[2:2 + 2 * np_]
    new_ref, o_ref, m_s, l_s, acc_s = refs[2 + 2 * np_:]
    j = pl.program_id(1)
    t = q_ref.shape[0]
    rows = GQA * t
    nl = sel_ref.shape[-1]
    scale = HEAD_DIM ** -0.5
    qpos = past + lax.broadcasted_iota(jnp.int32, (t, 1), 0)

    @pl.when(j == 0)
    def _():
        m_s[...] = jnp.full(m_s.shape, NEG_INF, F32)
        l_s[...] = jnp.zeros(l_s.shape, F32)
        acc_s[...] = jnp.zeros(acc_s.shape, F32)

    def update(kv, qs, k, v, k0):
        nk = k.shape[0]
        kpos = k0 + lax.broadcasted_iota(jnp.int32, (1, nk), 1)
        expand = jnp.where(lax.broadcasted_iota(jnp.int32, (nl, nk), 0) == kpos // SLC_BLOCK, 1.0, 0.0).astype(BF16)
        selk = jnp.dot(_pad_rows16(sel_ref[kv]).astype(BF16), expand, preferred_element_type=F32)[:t]
        allowed = (selk > 0.5) & (kpos <= qpos)
        s3 = jnp.where(allowed[None], (_dot_nt(qs, k) * scale).reshape(GQA, t, nk), NEG_INF)
        m_old = m_s[kv].reshape(GQA, t, 1)
        m_new = jnp.maximum(m_old, jnp.max(s3, axis=-1, keepdims=True))
        alpha = jnp.exp(m_old - m_new)
        p = jnp.exp(s3 - m_new) * allowed[None].astype(F32)
        l_s[kv] = (alpha * l_s[kv].reshape(GQA, t, 1) + jnp.sum(p, axis=-1, keepdims=True)).reshape(rows, 1)
        acc_s[kv] = (alpha.reshape(rows, 1) * acc_s[kv]
                     + jnp.dot(p.reshape(rows, nk).astype(BF16), v, preferred_element_type=F32))
        m_s[kv] = m_new.reshape(rows, 1)

    for kv in range(N_KV):
        qs = _stack_heads(q_ref, kv)
        k = jnp.concatenate([pages[r * 2][:, kv, :] for r in range(np_)], axis=0).astype(BF16)
        v = jnp.concatenate([pages[r * 2 + 1][:, kv, :] for r in range(np_)], axis=0).astype(BF16)
        update(kv, qs, k, v, j * (np_ * PAGE))

        @pl.when(j == n_steps - 1)
        def _():
            pad = jnp.zeros((LANES - t, HEAD_DIM), F32)
            k_n = jnp.concatenate([new_ref[:, kv * HEAD_DIM:(kv + 1) * HEAD_DIM], pad], axis=0).astype(BF16)
            v_n = jnp.concatenate([new_ref[:, (N_KV + kv) * HEAD_DIM:(N_KV + kv + 1) * HEAD_DIM], pad],
                                  axis=0).astype(BF16)
            update(kv, qs, k_n, v_n, past)
            o_ref[kv] = acc_s[kv] / jnp.maximum(l_s[kv], 1e-30)


def _nsa_sample_slc(p, sel, cache, page_table, layer, past):
    bsz, n_pages = page_table.shape
    t = p.shape[0] // bsz
    nl = sel.shape[-1]
    n_steps = n_pages // PAGES_PER_STEP
    cw = 2 * N_KV * HEAD_DIM
    rows = GQA * t
    in_specs = ([pl.BlockSpec((t, D_A), lambda b, j, pt: (b, 0)),
                 pl.BlockSpec((None, N_KV, t, nl), lambda b, j, pt: (b, 0, 0, 0))]
                + _page_specs(PAGES_PER_STEP, lambda j: j * PAGES_PER_STEP, layer, PAGES_PER_STEP, n_pages)
                + [pl.BlockSpec((t, cw), lambda b, j, pt: (b, KV_COL // cw + 1))])
    return pl.pallas_call(
        functools.partial(_nsa_sample_slc_body, n_steps=n_steps, past=past),
        grid_spec=pltpu.PrefetchScalarGridSpec(
            num_scalar_prefetch=1, grid=(bsz, n_steps), in_specs=in_specs,
            out_specs=pl.BlockSpec((None, N_KV, rows, HEAD_DIM), lambda b, j, pt: (b, 0, 0, 0)),
            scratch_shapes=[pltpu.VMEM((N_KV, rows, 1), F32), pltpu.VMEM((N_KV, rows, 1), F32),
                            pltpu.VMEM((N_KV, rows, HEAD_DIM), F32)]),
        out_shape=jax.ShapeDtypeStruct((bsz, N_KV, rows, HEAD_DIM), F32),
        compiler_params=_cparams(("parallel", "arbitrary")),
        name="nsa_sample_slc",
    )(page_table, p, sel, *([cache] * (2 * PAGES_PER_STEP)), p)


def _conformer_sample(p, state, bsz, conv_w, conv_b, ln_g, ln_b):
    t = p.shape[0] // bsz
    a_col = PB_COL // D_B
    hist = jnp.pad(state, ((0, 0), (CONV_HALO - (CONV_W - 1), 0), (0, 0)))
    wpad = jnp.pad(conv_w, ((0, CONV_HALO - CONV_W), (0, 0)))
    vec = pl.BlockSpec((1, D_B), lambda b, i: (0, 0))
    hspec = pl.BlockSpec((None, CONV_HALO, D_B), lambda b, i: (b, 0, 0))
    return pl.pallas_call(
        functools.partial(_conformer_body, tt=t, state_halo=True),
        grid=(bsz, 1),
        in_specs=[pl.BlockSpec((t, D_B), lambda b, i: (b, a_col)),
                  pl.BlockSpec((t, D_B), lambda b, i: (b, a_col + 1)),
                  hspec, hspec,
                  pl.BlockSpec((CONV_HALO, D_B), lambda b, i: (0, 0)), vec, vec, vec],
        out_specs=[pl.BlockSpec((t, D_B), lambda b, i: (b, 0)),
                   pl.BlockSpec((None, CONV_HALO, D_B), lambda b, i: (b, 0, 0))],
        out_shape=[jax.ShapeDtypeStruct((bsz * t, D_B), F32),
                   jax.ShapeDtypeStruct((bsz, CONV_HALO, D_B), F32)],
        scratch_shapes=[pltpu.VMEM((t + CONV_HALO, D_B), F32)],
        compiler_params=_cparams(("parallel", "arbitrary")),
        name="conformer_sample",
    )(p, p, hist, hist, wpad, conv_b.reshape(1, D_B), ln_g.reshape(1, D_B), ln_b.reshape(1, D_B))


def _gmlp_sample_body(u_ref, v_ref, lg_ref, lb_ref, w_ref, bs_ref, o_ref, vn_ref):
    t = u_ref.shape[0]
    u = jax.nn.gelu(u_ref[...])
    v = jax.nn.gelu(v_ref[...])
    mu = jnp.mean(v, axis=-1, keepdims=True)
    vc = v - mu
    var = jnp.mean(vc * vc, axis=-1, keepdims=True)
    vn = vc * lax.rsqrt(var + LN_EPS) * lg_ref[...] + lb_ref[...]
    vn_ref[...] = vn
    vb = vn.astype(BF16).astype(F32)
    tril = lax.broadcasted_iota(jnp.int32, (t, t), 0) >= lax.broadcasted_iota(jnp.int32, (t, t), 1)
    for h in range(N_HEADS_C):
        c = slice(h * HEAD_DIM, (h + 1) * HEAD_DIM)
        wm = jnp.where(tril, w_ref[h], 0.0).astype(BF16).astype(F32)
        mix = jnp.zeros((t, HEAD_DIM), F32)
        for s in range(t):
            mix = mix + wm[:, s:s + 1] * vb[s:s + 1, c]
        o_ref[:, c] = u[:, c] * (mix + bs_ref[:, h:h + 1])


def _gmlp_sample(p, bsz, ln_g, ln_b, sgu_w, sgu_b):
    t = p.shape[0] // bsz
    u_col = PC_COL // D_C
    vec = pl.BlockSpec((1, D_C), lambda b: (0, 0))
    row = pl.BlockSpec((t, D_C), lambda b: (b, 0))
    return pl.pallas_call(
        _gmlp_sample_body,
        grid=(bsz,),
        in_specs=[pl.BlockSpec((t, D_C), lambda b: (b, u_col)),
                  pl.BlockSpec((t, D_C), lambda b: (b, u_col + 1)),
                  vec, vec,
                  pl.BlockSpec((N_HEADS_C, t, t), lambda b: (0, 0, 0)),
                  pl.BlockSpec((t, N_HEADS_C), lambda b: (0, 0))],
        out_specs=[row, row],
        out_shape=[jax.ShapeDtypeStruct((bsz * t, D_C), F32), jax.ShapeDtypeStruct((bsz * t, D_C), F32)],
        compiler_params=_cparams(("parallel",)),
        name="gmlp_sample",
    )(p, p, ln_g.reshape(1, D_C), ln_b.reshape(1, D_C), sgu_w[:, :t, :t], sgu_b[:, :t].T)


def _mixer_sample(p, layer, cache_cmp, cache_slc, cache_win, state_conv, page_table, bsz,
                  cmp_pe, cmp_w, conv_w, conv_b, conv_ln_g, conv_ln_b, sgu_ln_g, sgu_ln_b, sgu_w, sgu_b):
    t = p.shape[0] // bsz
    n_pages = page_table.shape[1]
    past = n_pages * PAGE
    ns = -(-(past + t) // SLC_BLOCK)
    nl = -(-ns // LANES) * LANES
    kc = _compress_paged(p, cache_cmp, page_table, layer, cmp_pe, cmp_w.astype(BF16))
    o_cmp, o_win, sel = _nsa_sample_cw(p, kc, cache_win, layer, bsz, past, ns, nl)
    o_slc = _nsa_sample_slc(p, sel, cache_slc, page_table, layer, past)
    p3 = p.reshape(bsz, t, D_IN_PAD)
    gates = jax.nn.sigmoid(p3[:, :, GATE_COL:GATE_COL + N_BRANCH * N_HEADS_A]).reshape(bsz, t, N_BRANCH, N_HEADS_A)

    def heads(o):
        return o.reshape(bsz, N_KV, GQA, t, HEAD_DIM).transpose(0, 3, 1, 2, 4).reshape(bsz, t, N_HEADS_A, HEAD_DIM)

    o_a = (gates[:, :, 0, :, None] * heads(o_cmp) + gates[:, :, 1, :, None] * heads(o_slc)
           + gates[:, :, 2, :, None] * heads(o_win)).reshape(bsz * t, D_A)
    o_b, tail = _conformer_sample(p, state_conv[layer], bsz, conv_w, conv_b, conv_ln_g, conv_ln_b)
    o_c, vn = _gmlp_sample(p, bsz, sgu_ln_g, sgu_ln_b, sgu_w, sgu_b)
    cat = jnp.concatenate([o_a, o_b, o_c], axis=-1).astype(BF16)
    kv_row = (2, N_KV, HEAD_DIM)
    new_win = p3[:, :, KV_COL + 4 * KV_W:KV_COL + 6 * KV_W].reshape((bsz, t) + kv_row)
    new_state = (p3[:, :, KV_COL:KV_COL + 2 * KV_W].reshape((bsz, t) + kv_row),
                 p3[:, :, KV_COL + 2 * KV_W:KV_COL + 4 * KV_W].reshape((bsz, t) + kv_row),
                 jnp.concatenate([cache_win[layer][:, t:], new_win], axis=1),
                 tail[:, CONV_HALO - (CONV_W - 1):],
                 vn.reshape(bsz, t, D_C))
    return cat, new_state


def _pad_w_in(w):
    k = w.shape[0]
    g = N_BRANCH * N_HEADS_A
    return jnp.concatenate([
        w[:, :GATE_COL + g], jnp.zeros((k, PB_COL - GATE_COL - g), w.dtype), w[:, GATE_COL + g:]],
        axis=1).astype(BF16)


def kernel(x_prompt, x_sample, cache_kv_cmp, cache_kv_slc, cache_kv_win, state_conv, page_table,
           c_prompt, c_sample, w_in, w_out, cmp_pe, cmp_w, conv_w, conv_b, conv_ln_g, conv_ln_b,
           sgu_ln_g, sgu_ln_b, sgu_w, sgu_b, ada_w, ada_b, ln_g, ln_b, router_w, router_b,
           moe_w1, moe_b1, moe_w2, moe_b2):
    bp, tp, d = x_prompt.shape
    bs, ts, _ = x_sample.shape
    n_p, n_s = bp * tp, bs * ts
    xp = x_prompt.reshape(n_p, d)
    xs = x_sample.reshape(n_s, d)
    c_all = jnp.concatenate([c_prompt, c_sample], axis=0)
    c_all = jax.nn.silu(jnp.pad(c_all, ((0, 16 - bp - bs), (0, 0))))
    st = [[] for _ in range(9)]
    for l in range(DEPTH):
        mod = _matmul(c_all, ada_w, tm=16, tn=2048, tk=1024, bias=ada_b[l], w_layer=l)
        mod_p = mod[:bp].reshape(bp, 1, 6, d)
        mod_s = jnp.repeat(mod[bp:bp + bs], ts, axis=0).reshape(1, n_s, 6, d)
        mp = [mod_p[:, :, i] for i in range(6)]
        ms = [mod_s[:, :, i] for i in range(6)]
        w_in_l = _pad_w_in(w_in[l])
        w_out_l = w_out[l].astype(BF16)
        mix_w = (cmp_pe[l], cmp_w[l], conv_w[l], conv_b[l], conv_ln_g[l], conv_ln_b[l],
                 sgu_ln_g[l], sgu_ln_b[l], sgu_w[l], sgu_b[l])

        pp = _matmul(xp, w_in_l, tm=1024, tn=1024, tk=2048, mod=(mp[1], mp[0]))
        ps = _matmul(xs, w_in_l, tm=n_s, tn=1024, tk=512, mod=(ms[1], ms[0]))
        cat_p, st_p = _mixer_prompt(pp, bp, tp, *mix_w)
        cat_s, st_s = _mixer_sample(ps, l, cache_kv_cmp, cache_kv_slc, cache_kv_win, state_conv, page_table, bs, *mix_w)
        m_p = _matmul(cat_p, w_out_l, tm=1024, tn=1024, tk=2048)
        m_s = _matmul(cat_s, w_out_l, tm=n_s, tn=1024, tk=512)
        xp, xmp = _deepnorm(xp, m_p, mp[2], ln_g[l, 0], ln_b[l, 0], mod=(mp[4], mp[3]))
        xs, xms = _deepnorm(xs, m_s, ms[2], ln_g[l, 0], ln_b[l, 0], mod=(ms[4], ms[3]), tm=n_s)
        ys, pos = _moe(jnp.concatenate([xmp, xms], axis=0), l, router_w[l], router_b[l], moe_w1, moe_b1, moe_w2, moe_b2)
        xp = _deepnorm(xp, _expert_rows(ys, pos[:n_p]), mp[5], ln_g[l, 1], ln_b[l, 1], tm=128)
        xs = _deepnorm(xs, _expert_rows(ys, pos[n_p:]), ms[5], ln_g[l, 1], ln_b[l, 1], tm=n_s)
        for i in range(4):
            st[2 * i].append(st_p[i])
            st[2 * i + 1].append(st_s[i])
        st[8].append(st_s[4])
    return (xp.reshape(bp, tp, d), xs.reshape(bs, ts, d)) + tuple(jnp.stack(s) for s in st)
```

```python
import functools

import jax
import jax.numpy as jnp
import numpy as np
from jax import lax
from jax.experimental import pallas as pl
from jax.experimental.pallas import tpu as pltpu

F32 = jnp.float32
BF16 = jnp.bfloat16

D_MODEL = 4096
DEPTH = 2
HEAD_DIM = 128
D_A = D_MODEL // 2
D_B = D_MODEL // 4
D_C = D_MODEL // 4
D_MIX = D_A + D_B + D_C
N_HEADS_A = D_A // HEAD_DIM
N_KV = 2
GQA = N_HEADS_A // N_KV
KV_W = N_KV * HEAD_DIM
N_BRANCH = 3
N_HEADS_C = D_C // HEAD_DIM
CMP_BLOCK = 32
CMP_STRIDE = 16
SLC_BLOCK = 64
N_SLC = 16
WINDOW = 512
SEL_QBLOCK = 64
WIN_QBLOCK = 128
FORCE_BONUS = 1.0e4
CONV_W = 31
CHUNK = 128
N_EXPERTS = 32
TOP_K = 4
D_FF = D_MODEL // 2
SWIGLU_ALPHA = 1.702
SWIGLU_LIMIT = 7.0
DEEPNORM_ALPHA = (2 * DEPTH) ** 0.25
LN_EPS = 1e-5
NEG_INF = -1e30
SPLITS = (D_A, 6 * KV_W, N_BRANCH * N_HEADS_A, 2 * D_B, 2 * D_C)
D_IN = sum(SPLITS)

LANES = 128
V7X_VMEM_BYTES = 64 * 1024 * 1024
VMEM_LIMIT = 56 * 1024 * 1024
MOE_VMEM_LIMIT = 60 * 1024 * 1024

KV_COL = D_A
GATE_COL = D_A + 6 * KV_W
PB_COL = D_MODEL
PC_COL = PB_COL + 2 * D_B
D_IN_PAD = PC_COL + 2 * D_C

MOE_TM = 256


def _cparams(sem, vmem_limit=VMEM_LIMIT):
    return pltpu.CompilerParams(dimension_semantics=sem, vmem_limit_bytes=vmem_limit)


def _mm_body(*refs, n_k, modulate, has_bias):
    it = iter(refs)
    x_ref = next(it)
    sc_ref = next(it) if modulate else None
    sh_ref = next(it) if modulate else None
    w_ref = next(it)
    b_ref = next(it) if has_bias else None
    o_ref = next(it)
    acc_ref = next(it)
    k = pl.program_id(2)

    @pl.when(k == 0)
    def _():
        acc_ref[...] = jnp.zeros_like(acc_ref)

    x = x_ref[...]
    if modulate:
        x = x.astype(F32) * (1.0 + sc_ref[...]) + sh_ref[...]
    acc_ref[...] += jnp.dot(x.astype(BF16), w_ref[...].astype(BF16), preferred_element_type=F32)

    @pl.when(k == n_k - 1)
    def _():
        r = acc_ref[...]
        if has_bias:
            r = r + b_ref[...]
        o_ref[...] = r.astype(o_ref.dtype)


def _matmul(x, w, *, tm, tn, tk, out_dtype=F32, bias=None, mod=None, w_layer=None):
    m, kk = x.shape
    n = w.shape[-1]
    tm, tn, tk = min(tm, m), min(tn, n), min(tk, kk)
    assert m % tm == 0 and n % tn == 0 and kk % tk == 0
    n_k = kk // tk
    in_specs = [pl.BlockSpec((tm, tk), lambda i, j, k: (i, k))]
    args = [x]
    if mod is not None:
        g, r, _ = mod[0].shape
        rows = m // g
        assert rows % tm == 0 and r in (1, tm) and (r == 1 or rows == tm)
        per = rows // tm
        spec = pl.BlockSpec((None, r, tk), lambda i, j, k: (i // per, 0, k))
        in_specs += [spec, spec]
        args += [mod[0], mod[1]]
    if w_layer is None:
        in_specs.append(pl.BlockSpec((tk, tn), lambda i, j, k: (k, j)))
    else:
        in_specs.append(pl.BlockSpec((None, tk, tn), lambda i, j, k: (w_layer, k, j)))
    args.append(w)
    if bias is not None:
        in_specs.append(pl.BlockSpec((1, tn), lambda i, j, k: (0, j)))
        args.append(bias.reshape(1, n))
    return pl.pallas_call(
        functools.partial(_mm_body, n_k=n_k, modulate=mod is not None, has_bias=bias is not None),
        grid=(m // tm, n // tn, n_k),
        in_specs=in_specs,
        out_specs=pl.BlockSpec((tm, tn), lambda i, j, k: (i, j)),
        out_shape=jax.ShapeDtypeStruct((m, n), out_dtype),
        scratch_shapes=[pltpu.VMEM((tm, tn), F32)],
        compiler_params=_cparams(("parallel", "parallel", "arbitrary")),
        name="matmul",
    )(*args)


def _dn_body(*refs, modulate, n_terms):
    it = iter(refs)
    x_ref = next(it)
    m_refs = [next(it) for _ in range(n_terms)]
    gate_ref, g_ref, b_ref = (next(it) for _ in range(3))
    sc_ref = next(it) if modulate else None
    sh_ref = next(it) if modulate else None
    y_ref = next(it)
    ym_ref = next(it) if modulate else None
    m = m_refs[0][...]
    for m_ref in m_refs[1:]:
        m = m + m_ref[...]
    z = DEEPNORM_ALPHA * x_ref[...] + (1.0 + gate_ref[...]) * m
    mu = jnp.mean(z, axis=-1, keepdims=True)
    zc = z - mu
    var = jnp.mean(zc * zc, axis=-1, keepdims=True)
    y = zc * lax.rsqrt(var + LN_EPS) * g_ref[...] + b_ref[...]
    y_ref[...] = y
    if modulate:
        ym_ref[...] = (y * (1.0 + sc_ref[...]) + sh_ref[...]).astype(ym_ref.dtype)


def _deepnorm(x, m, gate, ln_g, ln_b, mod=None, *, tm=256):
    ms = list(m) if isinstance(m, (list, tuple)) else [m]
    n, d = x.shape
    tm = min(tm, n)
    assert n % tm == 0
    g, r, _ = gate.shape
    rows = n // g
    assert rows % tm == 0 and r in (1, tm) and (r == 1 or rows == tm)
    per = rows // tm
    row = pl.BlockSpec((tm, d), lambda i: (i, 0))
    grp = pl.BlockSpec((None, r, d), lambda i: (i // per, 0, 0))
    vec = pl.BlockSpec((1, d), lambda i: (0, 0))
    in_specs = [row] + [row] * len(ms) + [grp, vec, vec]
    args = [x] + ms + [gate, ln_g.reshape(1, d), ln_b.reshape(1, d)]
    out_specs = [row]
    out_shape = [jax.ShapeDtypeStruct((n, d), F32)]
    if mod is not None:
        in_specs += [grp, grp]
        args += [mod[0], mod[1]]
        out_specs.append(row)
        out_shape.append(jax.ShapeDtypeStruct((n, d), BF16))
    out = pl.pallas_call(
        functools.partial(_dn_body, modulate=mod is not None, n_terms=len(ms)),
        grid=(n // tm,),
        in_specs=in_specs,
        out_specs=out_specs,
        out_shape=out_shape,
        compiler_params=_cparams(("parallel",)),
        name="deepnorm",
    )(*args)
    return out if mod is not None else out[0]


def _expert_weights(be_ref, st_ref, nx_ref, lg_ref, w_hbm, wbuf, w_bf, sem, slot_s, *, layer, cols, tn):
    j, i = pl.program_id(0), pl.program_id(1)
    nj = pl.num_programs(0)

    def copies(e, jj, slot):
        return [pltpu.make_async_copy(w_hbm.at[layer, e, :, pl.ds(pl.multiple_of(c * tn, tn), tn)],
                                      wbuf.at[slot, part], sem.at[slot, part])
                for part, c in enumerate(cols(jj))]

    @pl.when(st_ref[i] == 1)
    def _():
        first = (j == 0) & (i == 0)

        @pl.when(first)
        def _():
            slot_s[0] = 0
            for c in copies(be_ref[i], j, 0):
                c.start()

        cur = slot_s[0]
        for c in copies(be_ref[i], j, cur):
            c.wait()
        for part in range(len(w_bf)):
            w_bf[part][...] = wbuf[cur, part].astype(BF16)
        last_grp = lg_ref[i] == 1

        @pl.when(jnp.logical_not(last_grp & (j == nj - 1)))
        def _():
            for c in copies(nx_ref[i], jnp.where(last_grp, j + 1, j), 1 - cur):
                c.start()

        slot_s[0] = 1 - cur


def _gmm1_body(be_ref, nu_ref, st_ref, nx_ref, lg_ref, x_ref, w_hbm, bg_ref, bl_ref, o_ref,
               wbuf, wg_s, wl_s, sem, slot_s, *, layer, nj, tn):
    i = pl.program_id(1)
    _expert_weights(be_ref, st_ref, nx_ref, lg_ref, w_hbm, wbuf, (wg_s, wl_s), sem, slot_s,
                    layer=layer, cols=lambda jj: (jj, nj + jj), tn=tn)

    @pl.when(i < nu_ref[0])
    def _():
        x = x_ref[...]
        hg = jnp.dot(x, wg_s[...], preferred_element_type=F32) + bg_ref[...]
        hl = jnp.dot(x, wl_s[...], preferred_element_type=F32) + bl_ref[...]
        hg = jnp.minimum(hg, SWIGLU_LIMIT)
        hl = jnp.clip(hl, -SWIGLU_LIMIT, SWIGLU_LIMIT)
        act = hg * jax.nn.sigmoid(SWIGLU_ALPHA * hg) * (hl + 1.0)
        o_ref[...] = act.astype(o_ref.dtype)

    @pl.when(i >= nu_ref[0])
    def _():
        o_ref[...] = jnp.zeros_like(o_ref)


def _gmm2_body(be_ref, nu_ref, st_ref, nx_ref, lg_ref, a_ref, w_hbm, b_ref, rw_ref, o_ref,
               wbuf, w_s, sem, slot_s, *, layer, tn):
    i = pl.program_id(1)
    _expert_weights(be_ref, st_ref, nx_ref, lg_ref, w_hbm, wbuf, (w_s,), sem, slot_s,
                    layer=layer, cols=lambda jj: (jj,), tn=tn)

    @pl.when(i < nu_ref[0])
    def _():
        y = jnp.dot(a_ref[...], w_s[...], preferred_element_type=F32) + b_ref[...]
        o_ref[...] = y * rw_ref[...]

    @pl.when(i >= nu_ref[0])
    def _():
        o_ref[...] = jnp.zeros_like(o_ref)


def _moe_experts(xs, row_w, blk_exp, n_used, groups, layer, w1, b1, w2, b2, *, tn1=512, tn2=2048):
    cap, d = xs.shape
    tm = MOE_TM
    n_blk = cap // tm
    nj1 = D_FF // tn1
    nj2 = d // tn2
    b1r = b1.reshape(DEPTH, N_EXPERTS, 1, 2 * D_FF)
    b2r = b2.reshape(DEPTH, N_EXPERTS, 1, d)
    hbm = pl.BlockSpec(memory_space=pl.ANY)

    def row_blk(i, nu):
        return jnp.minimum(i, nu[0] - 1)

    act = pl.pallas_call(
        functools.partial(_gmm1_body, layer=layer, nj=nj1, tn=tn1),
        grid_spec=pltpu.PrefetchScalarGridSpec(
            num_scalar_prefetch=5,
            grid=(nj1, n_blk),
            in_specs=[
                pl.BlockSpec((tm, d), lambda j, i, be, nu, *_: (row_blk(i, nu), 0)),
                hbm,
                pl.BlockSpec((None, None, 1, tn1), lambda j, i, be, nu, *_: (layer, be[i], 0, j)),
                pl.BlockSpec((None, None, 1, tn1), lambda j, i, be, nu, *_: (layer, be[i], 0, nj1 + j)),
            ],
            out_specs=pl.BlockSpec((tm, tn1), lambda j, i, be, nu, *_: (i, j)),
            scratch_shapes=[pltpu.VMEM((2, 2, d, tn1), F32), pltpu.VMEM((d, tn1), BF16), pltpu.VMEM((d, tn1), BF16),
                            pltpu.SemaphoreType.DMA((2, 2)), pltpu.SMEM((1,), jnp.int32)],
        ),
        out_shape=jax.ShapeDtypeStruct((cap, D_FF), BF16),
        compiler_params=_cparams(("arbitrary", "arbitrary"), MOE_VMEM_LIMIT),
        name="moe_gmm1",
    )(blk_exp, n_used, *groups, xs, w1, b1r, b1r)

    ys = pl.pallas_call(
        functools.partial(_gmm2_body, layer=layer, tn=tn2),
        grid_spec=pltpu.PrefetchScalarGridSpec(
            num_scalar_prefetch=5,
            grid=(nj2, n_blk),
            in_specs=[
                pl.BlockSpec((tm, D_FF), lambda j, i, be, nu, *_: (row_blk(i, nu), 0)),
                hbm,
                pl.BlockSpec((None, None, 1, tn2), lambda j, i, be, nu, *_: (layer, be[i], 0, j)),
                pl.BlockSpec((tm, 1), lambda j, i, be, nu, *_: (row_blk(i, nu), 0)),
            ],
            out_specs=pl.BlockSpec((tm, tn2), lambda j, i, be, nu, *_: (i, j)),
            scratch_shapes=[pltpu.VMEM((2, 1, D_FF, tn2), F32), pltpu.VMEM((D_FF, tn2), BF16),
                            pltpu.SemaphoreType.DMA((2, 1)), pltpu.SMEM((1,), jnp.int32)],
        ),
        out_shape=jax.ShapeDtypeStruct((cap, d), F32),
        compiler_params=_cparams(("arbitrary", "arbitrary"), MOE_VMEM_LIMIT),
        name="moe_gmm2",
    )(blk_exp, n_used, *groups, act, w2, b2r, row_w)
    return ys


def _route(logits):
    n = logits.shape[0]
    tm = MOE_TM
    cap = -(-(n * TOP_K + N_EXPERTS * (tm - 1)) // tm) * tm
    n_blk = cap // tm
    top_val, top_exp = lax.top_k(logits, TOP_K)
    weights = jax.nn.softmax(top_val, axis=-1)
    onehot = (top_exp[:, :, None] == jnp.arange(N_EXPERTS, dtype=jnp.int32)).any(axis=1).astype(jnp.int32)
    counts = onehot.sum(axis=0)
    padded = (counts + tm - 1) // tm * tm
    pad_end = jnp.cumsum(padded)
    pad_start = pad_end - padded
    rank = jnp.cumsum(onehot, axis=0) - onehot
    pos = pad_start[top_exp] + jnp.take_along_axis(rank, top_exp, axis=1)
    tok = jnp.broadcast_to(jnp.arange(n, dtype=jnp.int32)[:, None], (n, TOP_K))
    row_tok = jnp.zeros((cap,), jnp.int32).at[pos.reshape(-1)].set(tok.reshape(-1))
    row_w = jnp.zeros((cap,), F32).at[pos.reshape(-1)].set(weights.reshape(-1))
    blk_row = jnp.arange(n_blk, dtype=jnp.int32)[:, None] * tm
    blk_exp = jnp.minimum(jnp.sum((pad_end[None, :] <= blk_row).astype(jnp.int32), axis=1), N_EXPERTS - 1)
    n_used = (pad_end[-1:] // tm).astype(jnp.int32)
    blk = jnp.arange(n_blk, dtype=jnp.int32)
    prev_exp = jnp.concatenate([jnp.full((1,), -1, jnp.int32), blk_exp[:-1]])
    starts = ((blk < n_used[0]) & (blk_exp != prev_exp)).astype(jnp.int32)
    group_end = pad_end[blk_exp] // tm
    in_last = (group_end >= n_used[0]).astype(jnp.int32)
    next_exp = jnp.where(in_last == 1, blk_exp[0], blk_exp[jnp.minimum(group_end, n_blk - 1)]).astype(jnp.int32)
    return row_tok, row_w.reshape(cap, 1), blk_exp, n_used, (starts, next_exp, in_last), pos


def _moe(xm, layer, router_w, router_b, w1, b1, w2, b2):
    n, d = xm.shape
    rw = jnp.pad(router_w, ((0, 0), (0, LANES - N_EXPERTS))).astype(BF16)
    rb = jnp.pad(router_b, (0, LANES - N_EXPERTS))
    tm = next((c for c in range(min(n, 4096) // 16 * 16, 0, -16) if n % c == 0), n)
    logits = _matmul(xm, rw, tm=tm, tn=LANES, tk=1024, bias=rb)[:, :N_EXPERTS]
    row_tok, row_w, blk_exp, n_used, groups, pos = _route(logits)
    xs = jnp.take(xm, row_tok, axis=0, mode="clip")
    ys = _moe_experts(xs, row_w, blk_exp.astype(jnp.int32), n_used, groups, layer, w1, b1, w2, b2)
    return ys, pos


def _expert_rows(ys, pos):
    return [jnp.take(ys, pos[:, k], axis=0, mode="clip") for k in range(TOP_K)]


def _dot_nt(a, b):
    return lax.dot_general(a, b, (((1,), (1,)), ((), ())), preferred_element_type=F32)


def _compress_body(*refs, nch):
    x_refs, (pe_ref, w_ref, o_ref) = refs[:2 * N_KV], refs[2 * N_KV:]
    for grp in range(2 * N_KV):
        sel = grp // N_KV
        acc_a = jnp.zeros((nch, HEAD_DIM), F32)
        acc_b = jnp.zeros((nch, HEAD_DIM), F32)
        for s in range(CMP_STRIDE):
            x = x_refs[grp][pl.ds(s, nch, stride=CMP_STRIDE), :]
            xa = (x + pe_ref[sel, s:s + 1, :]).astype(BF16)
            xb = (x + pe_ref[sel, CMP_STRIDE + s:CMP_STRIDE + s + 1, :]).astype(BF16)
            acc_a += jnp.dot(xa, w_ref[sel, s * HEAD_DIM:(s + 1) * HEAD_DIM, :], preferred_element_type=F32)
            acc_b += jnp.dot(xb, w_ref[sel, (CMP_STRIDE + s) * HEAD_DIM:(CMP_STRIDE + s + 1) * HEAD_DIM, :],
                             preferred_element_type=F32)
        o_ref[grp] = acc_a + pltpu.roll(acc_b, nch - 1, axis=0)


def _compress_prompt(p, bsz, t, pe, w_bf):
    nch = t // CMP_STRIDE
    x_specs = [pl.BlockSpec((t, HEAD_DIM), functools.partial(lambda b, c: (b, c), c=KV_COL // HEAD_DIM + grp))
               for grp in range(2 * N_KV)]
    return pl.pallas_call(
        functools.partial(_compress_body, nch=nch),
        grid=(bsz,),
        in_specs=x_specs + [pl.BlockSpec((2, CMP_BLOCK, HEAD_DIM), lambda b: (0, 0, 0)),
                            pl.BlockSpec((2, CMP_BLOCK * HEAD_DIM, HEAD_DIM), lambda b: (0, 0, 0))],
        out_specs=pl.BlockSpec((None, 2 * N_KV, nch, HEAD_DIM), lambda b: (b, 0, 0, 0)),
        out_shape=jax.ShapeDtypeStruct((bsz, 2 * N_KV, nch, HEAD_DIM), F32),
        compiler_params=_cparams(("parallel",)),
        name="nsa_compress",
    )(*([p] * (2 * N_KV)), pe, w_bf)


def _softmax_rows(s3, mask):
    s3 = jnp.where(mask[None], s3, NEG_INF)
    m = jnp.max(s3, axis=-1, keepdims=True)
    p = jnp.exp(s3 - m) * mask[None].astype(F32)
    return p / jnp.maximum(jnp.sum(p, axis=-1, keepdims=True), 1e-30)


def _topk_mask(score, ns, n_sel):
    lane = lax.broadcasted_iota(jnp.int32, score.shape, 1)
    cnt = jnp.zeros(score.shape, F32)
    for i in range(ns):
        c = score[:, i:i + 1]
        beats = (c > score) | ((c == score) & (lane > i))
        cnt = cnt + jnp.where(beats, 1.0, 0.0)
    return jnp.where((cnt < n_sel) & (lane < ns), 1.0, 0.0)


def _block_scores(psum, qpos, ns, nblk_lanes):
    nc = psum.shape[1]
    ci = lax.broadcasted_iota(jnp.int32, (nc, nblk_lanes), 0) * CMP_STRIDE
    bj = lax.broadcasted_iota(jnp.int32, (nc, nblk_lanes), 1) * SLC_BLOCK
    cover = jnp.where((ci < bj + SLC_BLOCK) & (ci + CMP_BLOCK > bj), 1.0, 0.0).astype(BF16)
    hi = psum.astype(BF16)
    lo = (psum - hi.astype(F32)).astype(BF16)
    imp = jnp.dot(hi, cover, preferred_element_type=F32) + jnp.dot(lo, cover, preferred_element_type=F32)
    blk = lax.broadcasted_iota(jnp.int32, (1, nblk_lanes), 1)
    cur = qpos // SLC_BLOCK
    forced = (blk == 0) | (blk == cur) | (blk == cur - 1)
    score = jnp.where(blk <= cur, imp + jnp.where(forced, FORCE_BONUS, 0.0), NEG_INF)
    return jnp.where(blk < ns, score, -3.0e38)


def _nsa_prompt_body(q_ref, g_ref, kc_ref, slc_ref, win_ref, o_ref, slc_s, win_s, m_s, l_s, acc_s,
                     *, t, tq, tk, wk):
    i = pl.program_id(1)
    nch = t // CMP_STRIDE
    nc = nch - CMP_BLOCK // CMP_STRIDE + 1
    ns = t // SLC_BLOCK
    n_sel = min(N_SLC, ns)
    scale = HEAD_DIM ** -0.5
    rows = GQA * tq

    @pl.when(i == 0)
    def _():
        slc_s[...] = slc_ref[...].astype(BF16)
        win_s[...] = win_ref[...].astype(BF16)

    q0 = i * tq
    qpos = q0 + lax.broadcasted_iota(jnp.int32, (tq, 1), 0)
    gates = jax.nn.sigmoid(g_ref[...])
    for kv in range(N_KV):
        qs = jnp.concatenate([q_ref[:, (kv * GQA + g) * HEAD_DIM:(kv * GQA + g + 1) * HEAD_DIM]
                              for g in range(GQA)], axis=0).astype(BF16)
        kc = kc_ref[kv].astype(BF16)
        vc = kc_ref[N_KV + kv].astype(BF16)
        cidx = lax.broadcasted_iota(jnp.int32, (1, nch), 1)
        cmask = (cidx * CMP_STRIDE + CMP_BLOCK - 1 <= qpos) & (cidx < nc)
        p_c = _softmax_rows((_dot_nt(qs, kc) * scale).reshape(GQA, tq, nch), cmask)
        o_c = jnp.dot(p_c.reshape(rows, nch).astype(BF16), vc, preferred_element_type=F32)
        sel = _topk_mask(_block_scores(jnp.sum(p_c, axis=0), qpos, ns, LANES), ns, n_sel).astype(BF16)

        m_s[...] = jnp.full(m_s.shape, NEG_INF, F32)
        l_s[...] = jnp.zeros(l_s.shape, F32)
        acc_s[...] = jnp.zeros(acc_s.shape, F32)

        def chunk(c, carry):
            k0 = pl.multiple_of(c * tk, tk)
            k_c = slc_s[pl.ds(k0, tk), kv * HEAD_DIM:(kv + 1) * HEAD_DIM]
            v_c = slc_s[pl.ds(k0, tk), (N_KV + kv) * HEAD_DIM:(N_KV + kv + 1) * HEAD_DIM]
            kpos = k0 + lax.broadcasted_iota(jnp.int32, (1, tk), 1)
            expand = jnp.where(lax.broadcasted_iota(jnp.int32, (LANES, tk), 0) == kpos // SLC_BLOCK,
                               1.0, 0.0).astype(BF16)
            allowed = (jnp.dot(sel, expand, preferred_element_type=F32) > 0.5) & (kpos <= qpos)
            s3 = jnp.where(allowed[None], (_dot_nt(qs, k_c) * scale).reshape(GQA, tq, tk), NEG_INF)
            m_old = m_s[...].reshape(GQA, tq, 1)
            m_new = jnp.maximum(m_old, jnp.max(s3, axis=-1, keepdims=True))
            alpha = jnp.exp(m_old - m_new)
            p = jnp.exp(s3 - m_new) * allowed[None].astype(F32)
            l_s[...] = (alpha * l_s[...].reshape(GQA, tq, 1) + jnp.sum(p, axis=-1, keepdims=True)).reshape(rows, 1)
            acc_s[...] = (alpha.reshape(rows, 1) * acc_s[...]
                          + jnp.dot(p.reshape(rows, tk).astype(BF16), v_c, preferred_element_type=F32))
            m_s[...] = m_new.reshape(rows, 1)
            return carry

        lax.fori_loop(0, ((i + 1) * tq + tk - 1) // tk, chunk, 0)
        o_s = acc_s[...] / jnp.maximum(l_s[...], 1e-30)

        k0 = pl.multiple_of(jnp.clip(q0 - WINDOW, 0, t - wk), tq)
        k_w = win_s[pl.ds(k0, wk), kv * HEAD_DIM:(kv + 1) * HEAD_DIM]
        v_w = win_s[pl.ds(k0, wk), (N_KV + kv) * HEAD_DIM:(N_KV + kv + 1) * HEAD_DIM]
        kpos = k0 + lax.broadcasted_iota(jnp.int32, (1, wk), 1)
        wmask = (kpos <= qpos) & (kpos > qpos - WINDOW)
        p_w = _softmax_rows((_dot_nt(qs, k_w) * scale).reshape(GQA, tq, wk), wmask)
        o_w = jnp.dot(p_w.reshape(rows, wk).astype(BF16), v_w, preferred_element_type=F32)

        for g in range(GQA):
            h = kv * GQA + g
            r = slice(g * tq, (g + 1) * tq)
            o = (gates[:, h:h + 1] * o_c[r] + gates[:, N_HEADS_A + h:N_HEADS_A + h + 1] * o_s[r]
                 + gates[:, 2 * N_HEADS_A + h:2 * N_HEADS_A + h + 1] * o_w[r])
            o_ref[:, h * HEAD_DIM:(h + 1) * HEAD_DIM] = o.astype(o_ref.dtype)


def _nsa_prompt(p, kc, bsz, t, *, tq=128, tk=256):
    tq, tk = min(tq, t), min(tk, t)
    wk = min(t, WINDOW + tq)
    nq = t // tq
    nch = t // CMP_STRIDE
    cw = 2 * N_KV * HEAD_DIM
    rows = GQA * tq
    return pl.pallas_call(
        functools.partial(_nsa_prompt_body, t=t, tq=tq, tk=tk, wk=wk),
        grid=(bsz, nq),
        in_specs=[pl.BlockSpec((tq, D_A), lambda b, i: (b * nq + i, 0)),
                  pl.BlockSpec((tq, LANES), lambda b, i: (b * nq + i, GATE_COL // LANES)),
                  pl.BlockSpec((None, 2 * N_KV, nch, HEAD_DIM), lambda b, i: (b, 0, 0, 0)),
                  pl.BlockSpec((t, cw), lambda b, i: (b, KV_COL // cw + 1)),
                  pl.BlockSpec((t, cw), lambda b, i: (b, KV_COL // cw + 2))],
        out_specs=pl.BlockSpec((tq, D_A), lambda b, i: (b * nq + i, 0)),
        out_shape=jax.ShapeDtypeStruct((bsz * t, D_A), BF16),
        scratch_shapes=[pltpu.VMEM((t, cw), BF16), pltpu.VMEM((t, cw), BF16),
                        pltpu.VMEM((rows, 1), F32), pltpu.VMEM((rows, 1), F32), pltpu.VMEM((rows, HEAD_DIM), F32)],
        compiler_params=_cparams(("parallel", "arbitrary")),
        name="nsa_prompt",
    )(p, p, kc, p, p)


CONV_HALO = 32


def _conformer_body(a_ref, g_ref, ah_ref, gh_ref, w_ref, cb_ref, lg_ref, lb_ref, o_ref, tail_ref, ext_s,
                    *, tt, state_halo):
    glu = a_ref[...] * jax.nn.sigmoid(g_ref[...])
    if state_halo:
        ext_s[0:CONV_HALO, :] = ah_ref[...]
    else:
        halo = ah_ref[...] * jax.nn.sigmoid(gh_ref[...])
        ext_s[0:CONV_HALO, :] = jnp.where(pl.program_id(1) > 0, halo, 0.0)
    ext_s[CONV_HALO:, :] = glu
    y = jnp.zeros((tt, D_B), F32)
    for j in range(CONV_W):
        y = y + ext_s[pl.ds(CONV_HALO - (CONV_W - 1) + j, tt), :] * w_ref[j:j + 1, :]
    y = y + cb_ref[...]
    mu = jnp.mean(y, axis=-1, keepdims=True)
    yc = y - mu
    var = jnp.mean(yc * yc, axis=-1, keepdims=True)
    z = yc * lax.rsqrt(var + LN_EPS) * lg_ref[...] + lb_ref[...]
    o_ref[...] = (z * jax.nn.sigmoid(z)).astype(o_ref.dtype)
    tail_ref[...] = ext_s[tt:tt + CONV_HALO, :]


def _conformer_prompt(p, bsz, t, conv_w, conv_b, ln_g, ln_b, *, tt=256):
    tt = min(tt, t)
    nt = t // tt
    a_col = PB_COL // D_B
    per = tt // CONV_HALO
    wpad = jnp.pad(conv_w, ((0, CONV_HALO - CONV_W), (0, 0)))
    vec = pl.BlockSpec((1, D_B), lambda b, i: (0, 0))

    def halo_row(b, i):
        return jnp.maximum((b * nt + i) * per - 1, 0)

    return pl.pallas_call(
        functools.partial(_conformer_body, tt=tt, state_halo=False),
        grid=(bsz, nt),
        in_specs=[pl.BlockSpec((tt, D_B), lambda b, i: (b * nt + i, a_col)),
                  pl.BlockSpec((tt, D_B), lambda b, i: (b * nt + i, a_col + 1)),
                  pl.BlockSpec((CONV_HALO, D_B), lambda b, i: (halo_row(b, i), a_col)),
                  pl.BlockSpec((CONV_HALO, D_B), lambda b, i: (halo_row(b, i), a_col + 1)),
                  pl.BlockSpec((CONV_HALO, D_B), lambda b, i: (0, 0)), vec, vec, vec],
        out_specs=[pl.BlockSpec((tt, D_B), lambda b, i: (b * nt + i, 0)),
                   pl.BlockSpec((None, CONV_HALO, D_B), lambda b, i: (b, 0, 0))],
        out_shape=[jax.ShapeDtypeStruct((bsz * t, D_B), BF16),
                   jax.ShapeDtypeStruct((bsz, CONV_HALO, D_B), F32)],
        scratch_shapes=[pltpu.VMEM((tt + CONV_HALO, D_B), F32)],
        compiler_params=_cparams(("parallel", "arbitrary")),
        name="conformer",
    )(p, p, p, p, wpad, conv_b.reshape(1, D_B), ln_g.reshape(1, D_B), ln_b.reshape(1, D_B))


def _gmlp_body(u_ref, v_ref, lg_ref, lb_ref, w_ref, bs_ref, o_ref):
    u = jax.nn.gelu(u_ref[...])
    v = jax.nn.gelu(v_ref[...])
    mu = jnp.mean(v, axis=-1, keepdims=True)
    vc = v - mu
    var = jnp.mean(vc * vc, axis=-1, keepdims=True)
    vn = (vc * lax.rsqrt(var + LN_EPS) * lg_ref[...] + lb_ref[...]).astype(BF16)
    tril = lax.broadcasted_iota(jnp.int32, (CHUNK, CHUNK), 0) >= lax.broadcasted_iota(jnp.int32, (CHUNK, CHUNK), 1)
    for h in range(N_HEADS_C):
        c = slice(h * HEAD_DIM, (h + 1) * HEAD_DIM)
        wm = jnp.where(tril, w_ref[h], 0.0).astype(BF16)
        mix = jnp.dot(wm, vn[:, c], preferred_element_type=F32) + bs_ref[:, h:h + 1]
        o_ref[:, c] = (u[:, c] * mix).astype(o_ref.dtype)


def _gmlp_prompt(p, n, ln_g, ln_b, sgu_w, sgu_b):
    u_col = PC_COL // D_C
    vec = pl.BlockSpec((1, D_C), lambda r: (0, 0))
    return pl.pallas_call(
        _gmlp_body,
        grid=(n // CHUNK,),
        in_specs=[pl.BlockSpec((CHUNK, D_C), lambda r: (r, u_col)),
                  pl.BlockSpec((CHUNK, D_C), lambda r: (r, u_col + 1)),
                  vec, vec,
                  pl.BlockSpec((N_HEADS_C, CHUNK, CHUNK), lambda r: (0, 0, 0)),
                  pl.BlockSpec((CHUNK, N_HEADS_C), lambda r: (0, 0))],
        out_specs=pl.BlockSpec((CHUNK, D_C), lambda r: (r, 0)),
        out_shape=jax.ShapeDtypeStruct((n, D_C), BF16),
        compiler_params=_cparams(("parallel",)),
        name="gmlp",
    )(p, p, ln_g.reshape(1, D_C), ln_b.reshape(1, D_C), sgu_w, sgu_b.T)


def _mixer_prompt(p, bsz, t, cmp_pe, cmp_w, conv_w, conv_b, conv_ln_g, conv_ln_b, sgu_ln_g, sgu_ln_b, sgu_w, sgu_b):
    kc = _compress_prompt(p, bsz, t, cmp_pe, cmp_w.astype(BF16))
    o_a = _nsa_prompt(p, kc, bsz, t)
    o_b, tail = _conformer_prompt(p, bsz, t, conv_w, conv_b, conv_ln_g, conv_ln_b)
    o_c = _gmlp_prompt(p, bsz * t, sgu_ln_g, sgu_ln_b, sgu_w, sgu_b)
    cat = jnp.concatenate([o_a, o_b, o_c], axis=-1)
    p3 = p.reshape(bsz, t, D_IN_PAD)
    kv_row = (2, N_KV, HEAD_DIM)
    w_keep = min(WINDOW, t)
    new_state = (p3[:, :, KV_COL:KV_COL + 2 * KV_W].reshape((bsz, t) + kv_row),
                 p3[:, :, KV_COL + 2 * KV_W:KV_COL + 4 * KV_W].reshape((bsz, t) + kv_row),
                 p3[:, t - w_keep:, KV_COL + 4 * KV_W:KV_COL + 6 * KV_W].reshape((bsz, w_keep) + kv_row),
                 tail[:, CONV_HALO - (CONV_W - 1):])
    return cat, new_state


PAGE = 128
PAGES_PER_STEP = 16


def _pad_rows16(x):
    extra = (-x.shape[0]) % 16
    return x if extra == 0 else jnp.concatenate([x, jnp.zeros((extra,) + x.shape[1:], x.dtype)], axis=0)


def _page_specs(n, first, layer, per_step, n_pages):
    specs = []
    for r in range(n):
        for sel in range(2):
            def imap(b, j, pt, r=r, sel=sel):
                return (layer, pt[b, jnp.minimum(first(j) + r, n_pages - 1)], 0, sel, 0, 0)
            specs.append(pl.BlockSpec((None, None, PAGE, None, N_KV, HEAD_DIM), imap))
    return specs


def _compress_paged_body(pt_ref, *refs, n_steps):
    np_ = PAGES_PER_STEP
    ng = 2 * N_KV
    pages, nxt, tail = refs[:2 * np_], refs[2 * np_:2 * np_ + 2], refs[2 * np_ + 2:2 * np_ + 2 + ng]
    pe_ref, w_ref, o_ref, bm_s = refs[2 * np_ + 2 + ng:]
    last = pl.program_id(1) == n_steps - 1
    per_page = PAGE // CMP_STRIDE
    n_tok = np_ * per_page
    t_new = tail[0].shape[0]
    for grp in range(ng):
        sel, kvh = grp // N_KV, grp % N_KV
        acc_a = jnp.zeros((n_tok, HEAD_DIM), F32)
        acc_b = jnp.zeros((n_tok + per_page, HEAD_DIM), F32)
        for s in range(CMP_STRIDE):
            x = jnp.concatenate([pages[r * 2 + sel][pl.ds(s, per_page, stride=CMP_STRIDE), kvh, :]
                                 for r in range(np_)], axis=0)
            if s < t_new:
                new_rows = jnp.broadcast_to(tail[grp][s:s + 1, :], (per_page, HEAD_DIM))
            else:
                new_rows = jnp.zeros((per_page, HEAD_DIM), F32)
            nx = jnp.where(last, new_rows, nxt[sel][pl.ds(s, per_page, stride=CMP_STRIDE), kvh, :])
            xa = (x + pe_ref[sel, s:s + 1, :]).astype(BF16)
            xb = (jnp.concatenate([x, nx], axis=0) + pe_ref[sel, CMP_STRIDE + s:CMP_STRIDE + s + 1, :]).astype(BF16)
            acc_a += jnp.dot(xa, w_ref[sel, s * HEAD_DIM:(s + 1) * HEAD_DIM, :], preferred_element_type=F32)
            acc_b += jnp.dot(xb, w_ref[sel, (CMP_STRIDE + s) * HEAD_DIM:(CMP_STRIDE + s + 1) * HEAD_DIM, :],
                             preferred_element_type=F32)
        bm_s[...] = acc_b
        o_ref[grp] = acc_a + bm_s[pl.ds(1, n_tok), :]


def _compress_paged(p, cache, page_table, layer, pe, w_bf):
    bsz, n_pages = page_table.shape
    t_new = p.shape[0] // bsz
    assert cache.shape[2:] == (PAGE, 2, N_KV, HEAD_DIM) and n_pages % PAGES_PER_STEP == 0 and t_new <= CMP_STRIDE
    n_steps = n_pages // PAGES_PER_STEP
    n_tok = PAGES_PER_STEP * PAGE // CMP_STRIDE
    ng = 2 * N_KV
    first = lambda j: j * PAGES_PER_STEP
    in_specs = (_page_specs(PAGES_PER_STEP, first, layer, PAGES_PER_STEP, n_pages)
                + _page_specs(1, lambda j: (j + 1) * PAGES_PER_STEP, layer, PAGES_PER_STEP, n_pages)
                + [pl.BlockSpec((t_new, HEAD_DIM), functools.partial(lambda b, j, pt, c: (b, c), c=KV_COL // HEAD_DIM + g))
                   for g in range(ng)]
                + [pl.BlockSpec((2, CMP_BLOCK, HEAD_DIM), lambda b, j, pt: (0, 0, 0)),
                   pl.BlockSpec((2, CMP_BLOCK * HEAD_DIM, HEAD_DIM), lambda b, j, pt: (0, 0, 0))])
    return pl.pallas_call(
        functools.partial(_compress_paged_body, n_steps=n_steps),
        grid_spec=pltpu.PrefetchScalarGridSpec(
            num_scalar_prefetch=1, grid=(bsz, n_steps), in_specs=in_specs,
            out_specs=pl.BlockSpec((None, ng, n_tok, HEAD_DIM), lambda b, j, pt: (b, 0, j, 0)),
            scratch_shapes=[pltpu.VMEM((n_tok + PAGE // CMP_STRIDE, HEAD_DIM), F32)]),
        out_shape=jax.ShapeDtypeStruct((bsz, ng, n_pages * PAGE // CMP_STRIDE, HEAD_DIM), F32),
        compiler_params=_cparams(("parallel", "arbitrary")),
        name="nsa_compress_paged",
    )(page_table, *([cache] * (2 * (PAGES_PER_STEP + 1))), *([p] * ng), pe, w_bf)


def _stack_heads(q_ref, kv):
    return jnp.concatenate([q_ref[:, (kv * GQA + g) * HEAD_DIM:(kv * GQA + g + 1) * HEAD_DIM]
                            for g in range(GQA)], axis=0).astype(BF16)


def _nsa_sample_cw_body(q_ref, kc_ref, wk_ref, wv_ref, new_ref, oc_ref, ow_ref, sel_ref, *, past, w_buf, ns):
    t = q_ref.shape[0]
    nc = kc_ref.shape[1]
    rows = GQA * t
    scale = HEAD_DIM ** -0.5
    n_sel = min(N_SLC, ns)
    wk = w_buf + LANES
    qpos = past + lax.broadcasted_iota(jnp.int32, (t, 1), 0)
    for kv in range(N_KV):
        qs = _stack_heads(q_ref, kv)
        kc = kc_ref[kv].astype(BF16)
        vc = kc_ref[N_KV + kv].astype(BF16)
        cidx = lax.broadcasted_iota(jnp.int32, (1, nc), 1)
        cmask = cidx * CMP_STRIDE + CMP_BLOCK - 1 <= qpos
        p_c = _softmax_rows((_dot_nt(qs, kc) * scale).reshape(GQA, t, nc), cmask)
        oc_ref[kv] = jnp.dot(p_c.reshape(rows, nc).astype(BF16), vc, preferred_element_type=F32)
        nl = sel_ref.shape[-1]
        score = _block_scores(_pad_rows16(jnp.sum(p_c, axis=0)), _pad_rows16(qpos), ns, nl)
        sel_ref[kv] = _topk_mask(score, ns, n_sel)[:t]

        pad = jnp.zeros((LANES - t, HEAD_DIM), F32)
        k_w = jnp.concatenate([wk_ref[:, kv, :], new_ref[:, kv * HEAD_DIM:(kv + 1) * HEAD_DIM], pad],
                              axis=0).astype(BF16)
        v_w = jnp.concatenate([wv_ref[:, kv, :], new_ref[:, (N_KV + kv) * HEAD_DIM:(N_KV + kv + 1) * HEAD_DIM],
                               pad], axis=0).astype(BF16)
        kpos = past - w_buf + lax.broadcasted_iota(jnp.int32, (1, wk), 1)
        wmask = (kpos <= qpos) & (kpos > qpos - WINDOW) & (kpos >= 0)
        p_w = _softmax_rows((_dot_nt(qs, k_w) * scale).reshape(GQA, t, wk), wmask)
        ow_ref[kv] = jnp.dot(p_w.reshape(rows, wk).astype(BF16), v_w, preferred_element_type=F32)


def _nsa_sample_cw(p, kc, cache_win, layer, bsz, past, ns, nl):
    t = p.shape[0] // bsz
    w_buf = cache_win.shape[2]
    nc = kc.shape[2]
    cw = 2 * N_KV * HEAD_DIM
    rows = GQA * t
    win_specs = [pl.BlockSpec((None, None, w_buf, None, N_KV, HEAD_DIM),
                              functools.partial(lambda b, s: (layer, b, 0, s, 0, 0), s=sel)) for sel in range(2)]
    o_spec = pl.BlockSpec((None, N_KV, rows, HEAD_DIM), lambda b: (b, 0, 0, 0))
    return pl.pallas_call(
        functools.partial(_nsa_sample_cw_body, past=past, w_buf=w_buf, ns=ns),
        grid=(bsz,),
        in_specs=[pl.BlockSpec((t, D_A), lambda b: (b, 0)),
                  pl.BlockSpec((None, 2 * N_KV, nc, HEAD_DIM), lambda b: (b, 0, 0, 0))]
                 + win_specs + [pl.BlockSpec((t, cw), lambda b: (b, KV_COL // cw + 2))],
        out_specs=[o_spec, o_spec, pl.BlockSpec((None, N_KV, t, nl), lambda b: (b, 0, 0, 0))],
        out_shape=[jax.ShapeDtypeStruct((bsz, N_KV, rows, HEAD_DIM), F32),
                   jax.ShapeDtypeStruct((bsz, N_KV, rows, HEAD_DIM), F32),
                   jax.ShapeDtypeStruct((bsz, N_KV, t, nl), F32)],
        compiler_params=_cparams(("parallel",)),
        name="nsa_sample_cmp_win",
    )(p, kc, cache_win, cache_win, p)


def _nsa_sample_slc_body(pt_ref, *refs, n_steps, past):
    np_ = PAGES_PER_STEP
    ng = 2 * N_KV
    q_ref, sel_ref = refs[:2]
    pages = refs[2:2 + 2 * np_]
    new_ref, o_ref, m_s, l_s, acc_s = refs[2 + 2 * np_:]
    j = pl.program_id(1)
    t = q_ref.shape[0]
    rows = GQA * t
    nl = sel_ref.shape[-1]
    scale = HEAD_DIM ** -0.5
    qpos = past + lax.broadcasted_iota(jnp.int32, (t, 1), 0)

    @pl.when(j == 0)
    def _():
        m_s[...] = jnp.full(m_s.shape, NEG_INF, F32)
        l_s[...] = jnp.zeros(l_s.shape, F32)
        acc_s[...] = jnp.zeros(acc_s.shape, F32)

    def update(kv, qs, k, v, k0):
        nk = k.shape[0]
        kpos = k0 + lax.broadcasted_iota(jnp.int32, (1, nk), 1)
        expand = jnp.where(lax.broadcasted_iota(jnp.int32, (nl, nk), 0) == kpos // SLC_BLOCK, 1.0, 0.0).astype(BF16)
        selk = jnp.dot(_pad_rows16(sel_ref[kv]).astype(BF16), expand, preferred_element_type=F32)[:t]
        allowed = (selk > 0.5) & (kpos <= qpos)
        s3 = jnp.where(allowed[None], (_dot_nt(qs, k) * scale).reshape(GQA, t, nk), NEG_INF)
        m_old = m_s[kv].reshape(GQA, t, 1)
        m_new = jnp.maximum(m_old, jnp.max(s3, axis=-1, keepdims=True))
        alpha = jnp.exp(m_old - m_new)
        p = jnp.exp(s3 - m_new) * allowed[None].astype(F32)
        l_s[kv] = (alpha * l_s[kv].reshape(GQA, t, 1) + jnp.sum(p, axis=-1, keepdims=True)).reshape(rows, 1)
        acc_s[kv] = (alpha.reshape(rows, 1) * acc_s[kv]
                     + jnp.dot(p.reshape(rows, nk).astype(BF16), v, preferred_element_type=F32))
        m_s[kv] = m_new.reshape(rows, 1)

    for kv in range(N_KV):
        qs = _stack_heads(q_ref, kv)
        k = jnp.concatenate([pages[r * 2][:, kv, :] for r in range(np_)], axis=0).astype(BF16)
        v = jnp.concatenate([pages[r * 2 + 1][:, kv, :] for r in range(np_)], axis=0).astype(BF16)
        update(kv, qs, k, v, j * (np_ * PAGE))

        @pl.when(j == n_steps - 1)
        def _():
            pad = jnp.zeros((LANES - t, HEAD_DIM), F32)
            k_n = jnp.concatenate([new_ref[:, kv * HEAD_DIM:(kv + 1) * HEAD_DIM], pad], axis=0).astype(BF16)
            v_n = jnp.concatenate([new_ref[:, (N_KV + kv) * HEAD_DIM:(N_KV + kv + 1) * HEAD_DIM], pad],
                                  axis=0).astype(BF16)
            update(kv, qs, k_n, v_n, past)
            o_ref[kv] = acc_s[kv] / jnp.maximum(l_s[kv], 1e-30)


def _nsa_sample_slc(p, sel, cache, page_table, layer, past):
    bsz, n_pages = page_table.shape
    t = p.shape[0] // bsz
    nl = sel.shape[-1]
    n_steps = n_pages // PAGES_PER_STEP
    cw = 2 * N_KV * HEAD_DIM
    rows = GQA * t
    in_specs = ([pl.BlockSpec((t, D_A), lambda b, j, pt: (b, 0)),
                 pl.BlockSpec((None, N_KV, t, nl), lambda b, j, pt: (b, 0, 0, 0))]
                + _page_specs(PAGES_PER_STEP, lambda j: j * PAGES_PER_STEP, layer, PAGES_PER_STEP, n_pages)
                + [pl.BlockSpec((t, cw), lambda b, j, pt: (b, KV_COL // cw + 1))])
    return pl.pallas_call(
        functools.partial(_nsa_sample_slc_body, n_steps=n_steps, past=past),
        grid_spec=pltpu.PrefetchScalarGridSpec(
            num_scalar_prefetch=1, grid=(bsz, n_steps), in_specs=in_specs,
            out_specs=pl.BlockSpec((None, N_KV, rows, HEAD_DIM), lambda b, j, pt: (b, 0, 0, 0)),
            scratch_shapes=[pltpu.VMEM((N_KV, rows, 1), F32), pltpu.VMEM((N_KV, rows, 1), F32),
                            pltpu.VMEM((N_KV, rows, HEAD_DIM), F32)]),
        out_shape=jax.ShapeDtypeStruct((bsz, N_KV, rows, HEAD_DIM), F32),
        compiler_params=_cparams(("parallel", "arbitrary")),
        name="nsa_sample_slc",
    )(page_table, p, sel, *([cache] * (2 * PAGES_PER_STEP)), p)


def _conformer_sample(p, state, bsz, conv_w, conv_b, ln_g, ln_b):
    t = p.shape[0] // bsz
    a_col = PB_COL // D_B
    hist = jnp.pad(state, ((0, 0), (CONV_HALO - (CONV_W - 1), 0), (0, 0)))
    wpad = jnp.pad(conv_w, ((0, CONV_HALO - CONV_W), (0, 0)))
    vec = pl.BlockSpec((1, D_B), lambda b, i: (0, 0))
    hspec = pl.BlockSpec((None, CONV_HALO, D_B), lambda b, i: (b, 0, 0))
    return pl.pallas_call(
        functools.partial(_conformer_body, tt=t, state_halo=True),
        grid=(bsz, 1),
        in_specs=[pl.BlockSpec((t, D_B), lambda b, i: (b, a_col)),
                  pl.BlockSpec((t, D_B), lambda b, i: (b, a_col + 1)),
                  hspec, hspec,
                  pl.BlockSpec((CONV_HALO, D_B), lambda b, i: (0, 0)), vec, vec, vec],
        out_specs=[pl.BlockSpec((t, D_B), lambda b, i: (b, 0)),
                   pl.BlockSpec((None, CONV_HALO, D_B), lambda b, i: (b, 0, 0))],
        out_shape=[jax.ShapeDtypeStruct((bsz * t, D_B), F32),
                   jax.ShapeDtypeStruct((bsz, CONV_HALO, D_B), F32)],
        scratch_shapes=[pltpu.VMEM((t + CONV_HALO, D_B), F32)],
        compiler_params=_cparams(("parallel", "arbitrary")),
        name="conformer_sample",
    )(p, p, hist, hist, wpad, conv_b.reshape(1, D_B), ln_g.reshape(1, D_B), ln_b.reshape(1, D_B))


def _gmlp_sample_body(u_ref, v_ref, lg_ref, lb_ref, w_ref, bs_ref, o_ref, vn_ref):
    t = u_ref.shape[0]
    u = jax.nn.gelu(u_ref[...])
    v = jax.nn.gelu(v_ref[...])
    mu = jnp.mean(v, axis=-1, keepdims=True)
    vc = v - mu
    var = jnp.mean(vc * vc, axis=-1, keepdims=True)
    vn = vc * lax.rsqrt(var + LN_EPS) * lg_ref[...] + lb_ref[...]
    vn_ref[...] = vn
    vb = vn.astype(BF16).astype(F32)
    tril = lax.broadcasted_iota(jnp.int32, (t, t), 0) >= lax.broadcasted_iota(jnp.int32, (t, t), 1)
    for h in range(N_HEADS_C):
        c = slice(h * HEAD_DIM, (h + 1) * HEAD_DIM)
        wm = jnp.where(tril, w_ref[h], 0.0).astype(BF16).astype(F32)
        mix = jnp.zeros((t, HEAD_DIM), F32)
        for s in range(t):
            mix = mix + wm[:, s:s + 1] * vb[s:s + 1, c]
        o_ref[:, c] = u[:, c] * (mix + bs_ref[:, h:h + 1])


def _gmlp_sample(p, bsz, ln_g, ln_b, sgu_w, sgu_b):
    t = p.shape[0] // bsz
    u_col = PC_COL // D_C
    vec = pl.BlockSpec((1, D_C), lambda b: (0, 0))
    row = pl.BlockSpec((t, D_C), lambda b: (b, 0))
    return pl.pallas_call(
        _gmlp_sample_body,
        grid=(bsz,),
        in_specs=[pl.BlockSpec((t, D_C), lambda b: (b, u_col)),
                  pl.BlockSpec((t, D_C), lambda b: (b, u_col + 1)),
                  vec, vec,
                  pl.BlockSpec((N_HEADS_C, t, t), lambda b: (0, 0, 0)),
                  pl.BlockSpec((t, N_HEADS_C), lambda b: (0, 0))],
        out_specs=[row, row],
        out_shape=[jax.ShapeDtypeStruct((bsz * t, D_C), F32), jax.ShapeDtypeStruct((bsz * t, D_C), F32)],
        compiler_params=_cparams(("parallel",)),
        name="gmlp_sample",
    )(p, p, ln_g.reshape(1, D_C), ln_b.reshape(1, D_C), sgu_w[:, :t, :t], sgu_b[:, :t].T)


def _mixer_sample(p, layer, cache_cmp, cache_slc, cache_win, state_conv, page_table, bsz,
                  cmp_pe, cmp_w, conv_w, conv_b, conv_ln_g, conv_ln_b, sgu_ln_g, sgu_ln_b, sgu_w, sgu_b):
    t = p.shape[0] // bsz
    n_pages = page_table.shape[1]
    past = n_pages * PAGE
    ns = -(-(past + t) // SLC_BLOCK)
    nl = -(-ns // LANES) * LANES
    kc = _compress_paged(p, cache_cmp, page_table, layer, cmp_pe, cmp_w.astype(BF16))
    o_cmp, o_win, sel = _nsa_sample_cw(p, kc, cache_win, layer, bsz, past, ns, nl)
    o_slc = _nsa_sample_slc(p, sel, cache_slc, page_table, layer, past)
    p3 = p.reshape(bsz, t, D_IN_PAD)
    gates = jax.nn.sigmoid(p3[:, :, GATE_COL:GATE_COL + N_BRANCH * N_HEADS_A]).reshape(bsz, t, N_BRANCH, N_HEADS_A)

    def heads(o):
        return o.reshape(bsz, N_KV, GQA, t, HEAD_DIM).transpose(0, 3, 1, 2, 4).reshape(bsz, t, N_HEADS_A, HEAD_DIM)

    o_a = (gates[:, :, 0, :, None] * heads(o_cmp) + gates[:, :, 1, :, None] * heads(o_slc)
           + gates[:, :, 2, :, None] * heads(o_win)).reshape(bsz * t, D_A)
    o_b, tail = _conformer_sample(p, state_conv[layer], bsz, conv_w, conv_b, conv_ln_g, conv_ln_b)
    o_c, vn = _gmlp_sample(p, bsz, sgu_ln_g, sgu_ln_b, sgu_w, sgu_b)
    cat = jnp.concatenate([o_a, o_b, o_c], axis=-1).astype(BF16)
    kv_row = (2, N_KV, HEAD_DIM)
    new_win = p3[:, :, KV_COL + 4 * KV_W:KV_COL + 6 * KV_W].reshape((bsz, t) + kv_row)
    new_state = (p3[:, :, KV_COL:KV_COL + 2 * KV_W].reshape((bsz, t) + kv_row),
                 p3[:, :, KV_COL + 2 * KV_W:KV_COL + 4 * KV_W].reshape((bsz, t) + kv_row),
                 jnp.concatenate([cache_win[layer][:, t:], new_win], axis=1),
                 tail[:, CONV_HALO - (CONV_W - 1):],
                 vn.reshape(bsz, t, D_C))
    return cat, new_state


def _pad_w_in(w):
    k = w.shape[0]
    g = N_BRANCH * N_HEADS_A
    return jnp.concatenate([
        w[:, :GATE_COL + g], jnp.zeros((k, PB_COL - GATE_COL - g), w.dtype), w[:, GATE_COL + g:]],
        axis=1).astype(BF16)


def kernel(x_prompt, x_sample, cache_kv_cmp, cache_kv_slc, cache_kv_win, state_conv, page_table,
           c_prompt, c_sample, w_in, w_out, cmp_pe, cmp_w, conv_w, conv_b, conv_ln_g, conv_ln_b,
           sgu_ln_g, sgu_ln_b, sgu_w, sgu_b, ada_w, ada_b, ln_g, ln_b, router_w, router_b,
           moe_w1, moe_b1, moe_w2, moe_b2):
    bp, tp, d = x_prompt.shape
    bs, ts, _ = x_sample.shape
    n_p, n_s = bp * tp, bs * ts
    xp = x_prompt.reshape(n_p, d)
    xs = x_sample.reshape(n_s, d)
    c_all = jnp.concatenate([c_prompt, c_sample], axis=0)
    c_all = jax.nn.silu(jnp.pad(c_all, ((0, 16 - bp - bs), (0, 0))))
    st = [[] for _ in range(9)]
    for l in range(DEPTH):
        mod = _matmul(c_all, ada_w, tm=16, tn=2048, tk=1024, bias=ada_b[l], w_layer=l)
        mod_p = mod[:bp].reshape(bp, 1, 6, d)
        mod_s = jnp.repeat(mod[bp:bp + bs], ts, axis=0).reshape(1, n_s, 6, d)
        mp = [mod_p[:, :, i] for i in range(6)]
        ms = [mod_s[:, :, i] for i in range(6)]
        w_in_l = _pad_w_in(w_in[l])
        w_out_l = w_out[l].astype(BF16)
        mix_w = (cmp_pe[l], cmp_w[l], conv_w[l], conv_b[l], conv_ln_g[l], conv_ln_b[l],
                 sgu_ln_g[l], sgu_ln_b[l], sgu_w[l], sgu_b[l])

        pp = _matmul(xp, w_in_l, tm=1024, tn=1024, tk=2048, mod=(mp[1], mp[0]))
        ps = _matmul(xs, w_in_l, tm=n_s, tn=1024, tk=512, mod=(ms[1], ms[0]))
        cat_p, st_p = _mixer_prompt(pp, bp, tp, *mix_w)
        cat_s, st_s = _mixer_sample(ps, l, cache_kv_cmp, cache_kv_slc, cache_kv_win, state_conv, page_table, bs, *mix_w)
        m_p = _matmul(cat_p, w_out_l, tm=1024, tn=1024, tk=2048)
        m_s = _matmul(cat_s, w_out_l, tm=n_s, tn=1024, tk=512)
        xp, xmp = _deepnorm(xp, m_p, mp[2], ln_g[l, 0], ln_b[l, 0], mod=(mp[4], mp[3]))
        xs, xms = _deepnorm(xs, m_s, ms[2], ln_g[l, 0], ln_b[l, 0], mod=(ms[4], ms[3]), tm=n_s)
        ys, pos = _moe(jnp.concatenate([xmp, xms], axis=0), l, router_w[l], router_b[l], moe_w1, moe_b1, moe_w2, moe_b2)
        xp = _deepnorm(xp, _expert_rows(ys, pos[:n_p]), mp[5], ln_g[l, 1], ln_b[l, 1], tm=128)
        xs = _deepnorm(xs, _expert_rows(ys, pos[n_p:]), ms[5], ln_g[l, 1], ln_b[l, 1], tm=n_s)
        for i in range(4):
            st[2 * i].append(st_p[i])
            st[2 * i + 1].append(st_s[i])
        st[8].append(st_s[4])
    return (xp.reshape(bp, tp, d), xs.reshape(bs, ts, d)) + tuple(jnp.stack(s) for s in st)
```
